```python
import math
import jax, jax.numpy as jnp
from jax import lax
import numpy as np


D_MODEL = 1024
BATCH = 1
SEQ = 16384
DEPTH = 2

N_META = 16
CHUNK = 128
PAD = CHUNK - N_META
EPS = 1e-6

CONV_A_WIDTH = D_MODEL
CONV_A_K = 3
SSD_HEAD_DIM = 64
SSD_HEADS = D_MODEL // SSD_HEAD_DIM
SSD_INNER = SSD_HEADS * SSD_HEAD_DIM
SSD_GROUPS = 4
SSD_STATE = 128
SSD_CONV_K = 4
SSD_CONV_DIM = SSD_INNER + 2 * SSD_GROUPS * SSD_STATE
RET_HEADS = 4
RET_QK_DIM = 256
RET_V_DIM = D_MODEL // RET_HEADS
RET_WIDTH = RET_HEADS * RET_V_DIM
ROPE_BASE = 10000.0
SB_HEADS = 8
SB_HEAD_DIM = D_MODEL // SB_HEADS
SB_WIDTH = SB_HEADS * SB_HEAD_DIM
N_BRANCH = 4
BRANCH_WIDTH = D_MODEL
D_FF = ((8 * D_MODEL // 3 + 255) // 256) * 256

IN_SIZES = (
    CONV_A_WIDTH, CONV_A_WIDTH, CONV_A_WIDTH,
    SSD_INNER, SSD_CONV_DIM, SSD_HEADS,
    RET_HEADS * RET_QK_DIM, RET_HEADS * RET_QK_DIM,
    RET_WIDTH, RET_WIDTH,
    SB_WIDTH, SB_WIDTH, SB_WIDTH,
    N_BRANCH * D_MODEL,
)
IN_WIDTH = sum(IN_SIZES)
IN_SPLITS = tuple(np.cumsum(IN_SIZES)[:-1].tolist())

kernel_name = 'hybrid_gated_conv_ssd_retention_stickbreaking'


def rmsnorm(x, w):
    xf = x.astype(jnp.float32)
    y = xf * lax.rsqrt(jnp.mean(xf * xf, axis=-1, keepdims=True) + EPS)
    return (y * w.astype(jnp.float32)).astype(x.dtype)


def causal_dwconv(u, w):
    k_taps = w.shape[0]
    length = u.shape[1]
    up = jnp.pad(u, ((0, 0), (k_taps - 1, 0), (0, 0)))
    out = up[:, 0:length] * w[0]
    for i in range(1, k_taps):
        out = out + up[:, i:i + length] * w[i]
    return out


def short_conv_mixer(b_gate, c_gate, xa, conv_w, valid):
    vm = valid[None, :, None].astype(xa.dtype)
    u = c_gate * xa * vm
    return (b_gate * causal_dwconv(u, conv_w)).astype(xa.dtype)


def ssd_mixer(z, xbc, dt_raw, conv_w, conv_b, dt_bias, a_log, d_skip, norm_w, valid):
    f32 = jnp.float32
    b, L, _ = z.shape
    nc = L // CHUNK
    hpg = SSD_HEADS // SSD_GROUPS
    vm = valid[None, :, None].astype(xbc.dtype)
    xbc = jax.nn.silu(causal_dwconv(xbc * vm, conv_w) + conv_b)
    xs, bm, cm = jnp.split(xbc, (SSD_INNER, SSD_INNER + SSD_GROUPS * SSD_STATE), axis=-1)
    xs = (xs * vm).astype(f32).reshape(b, nc, CHUNK, SSD_GROUPS, hpg, SSD_HEAD_DIM)
    bc = bm.astype(f32).reshape(b, nc, CHUNK, SSD_GROUPS, SSD_STATE)
    cc = cm.astype(f32).reshape(b, nc, CHUNK, SSD_GROUPS, SSD_STATE)
    dt = jax.nn.softplus(dt_raw.astype(f32) + dt_bias.astype(f32))
    a = (-jnp.exp(a_log.astype(f32)) * dt).reshape(b, nc, CHUNK, SSD_GROUPS, hpg)
    xdt = xs * dt.reshape(b, nc, CHUNK, SSD_GROUPS, hpg)[..., None]
    acs = jnp.moveaxis(jnp.cumsum(a, axis=2), 2, -1)
    causal = jnp.tril(jnp.ones((CHUNK, CHUNK), dtype=bool))
    seg = jnp.exp(jnp.where(causal, acs[..., :, None] - acs[..., None, :], -jnp.inf))
    cb = jnp.einsum('bclgn,bcsgn->bcgls', cc, bc)
    y_diag = jnp.einsum('bcgjls,bcsgjp->bclgjp', cb[:, :, :, None] * seg, xdt)
    decay_states = jnp.exp(acs[..., -1:] - acs)
    states = jnp.einsum('bclgn,bcgjl,bclgjp->bcgjpn', bc, decay_states, xdt)
    chunk_decay = jnp.exp(acs[..., -1])

    def step(hstate, inp):
        st, dec = inp
        return hstate * dec[..., None, None] + st, hstate

    h0 = jnp.zeros((b, SSD_GROUPS, hpg, SSD_HEAD_DIM, SSD_STATE), f32)
    _, prev = lax.scan(step, h0, (jnp.moveaxis(states, 1, 0), jnp.moveaxis(chunk_decay, 1, 0)))
    prev = jnp.moveaxis(prev, 0, 1)
    y_off = jnp.einsum('bclgn,bcgjpn,bcgjl->bclgjp', cc, prev, jnp.exp(acs))
    y = y_diag + y_off + xs * d_skip.astype(f32).reshape(SSD_GROUPS, hpg)[..., None]
    y = y.reshape(b, L, SSD_INNER) * jax.nn.silu(z.astype(f32))
    yg = y.reshape(b, L, SSD_GROUPS, SSD_INNER // SSD_GROUPS)
    yg = yg * lax.rsqrt(jnp.mean(yg * yg, axis=-1, keepdims=True) + EPS)
    return (yg.reshape(b, L, SSD_INNER) * norm_w.astype(f32)).astype(z.dtype)


def rotate(x, pos):
    half = x.shape[-1] // 2
    inv = ROPE_BASE ** (-jnp.arange(half, dtype=jnp.float32) / half)
    ang = pos.astype(jnp.float32)[:, None] * inv[None, :]
    cos = jnp.cos(ang)[None, :, None, :]
    sin = jnp.sin(ang)[None, :, None, :]
    x1, x2 = x[..., :half], x[..., half:]
    return jnp.concatenate([x1 * cos - x2 * sin, x1 * sin + x2 * cos], axis=-1)


def retention_mixer(q, k, v, g, valid):
    f32 = jnp.float32
    b, L, _ = q.shape
    nc = L // CHUNK
    pos = jnp.arange(L)
    qr = rotate(q.astype(f32).reshape(b, L, RET_HEADS, RET_QK_DIM), pos)
    kr = rotate(k.astype(f32).reshape(b, L, RET_HEADS, RET_QK_DIM), pos) * (RET_QK_DIM ** -0.5)
    vr = v.astype(f32).reshape(b, L, RET_HEADS, RET_V_DIM) * valid.astype(f32)[None, :, None, None]
    log_gamma = jnp.log(1.0 - jnp.power(2.0, -5.0 - jnp.arange(RET_HEADS, dtype=f32)))
    idx = jnp.arange(CHUNK, dtype=f32)
    rel = idx[:, None] - idx[None, :]
    dmask = jnp.where(rel >= 0, jnp.exp(log_gamma[:, None, None] * jnp.maximum(rel, 0.0)), 0.0)
    qc = qr.reshape(b, nc, CHUNK, RET_HEADS, RET_QK_DIM)
    kc = kr.reshape(b, nc, CHUNK, RET_HEADS, RET_QK_DIM)
    vc = vr.reshape(b, nc, CHUNK, RET_HEADS, RET_V_DIM)
    scores = jnp.einsum('bclhd,bcshd->bchls', qc, kc) * dmask
    y_in = jnp.einsum('bchls,bcshe->bclhe', scores, vc)
    k_decay = jnp.exp(log_gamma[:, None] * (CHUNK - 1 - idx)[None, :])
    kv = jnp.einsum('bcshd,hs,bcshe->bchde', kc, k_decay, vc)
    chunk_decay = jnp.exp(log_gamma * CHUNK)

    def step(r, kv_c):
        return r * chunk_decay[:, None, None] + kv_c, r

    r0 = jnp.zeros((b, RET_HEADS, RET_QK_DIM, RET_V_DIM), f32)
    _, prev = lax.scan(step, r0, jnp.moveaxis(kv, 1, 0))
    prev = jnp.moveaxis(prev, 0, 1)
    q_decay = jnp.exp(log_gamma[None, :] * (idx + 1.0)[:, None])
    y_cr = jnp.einsum('bclhd,bchde->bclhe', qc, prev) * q_decay[:, :, None]
    y = (y_in + y_cr).reshape(b, L, RET_HEADS, RET_V_DIM)
    mu = jnp.mean(y, axis=-1, keepdims=True)
    var = jnp.mean(jnp.square(y - mu), axis=-1, keepdims=True)
    y = ((y - mu) * lax.rsqrt(var + EPS)).reshape(b, L, RET_WIDTH)
    return (y * jax.nn.silu(g.astype(f32))).astype(q.dtype)


def stick_breaking_mixer(q, k, v, valid):
    f32 = jnp.float32
    b, L, _ = q.shape
    nb = L // CHUNK
    qh = q.reshape(b, L, SB_HEADS, SB_HEAD_DIM)
    kh = k.reshape(b, L, SB_HEADS, SB_HEAD_DIM)
    vh = v.reshape(b, L, SB_HEADS, SB_HEAD_DIM)
    qb = jnp.moveaxis(qh.reshape(b, nb, CHUNK, SB_HEADS, SB_HEAD_DIM), 1, 0)
    key_pos = jnp.arange(L)
    scale = SB_HEAD_DIM ** -0.5

    def block(args):
        q_blk, i = args
        t = i * CHUNK + jnp.arange(CHUNK)
        z = jnp.einsum('bthd,bshd->bhts', q_blk, kh).astype(f32) * scale
        m = (key_pos[None, :] < t[:, None]) & valid[None, :]
        l_neg = jnp.where(m, jax.nn.log_sigmoid(-z), 0.0)
        log_w = jax.nn.log_sigmoid(z) + lax.cumsum(l_neg, axis=3, reverse=True) - l_neg
        w = jnp.where(m, jnp.exp(log_w), 0.0)
        return jnp.einsum('bhts,bshe->bthe', w.astype(vh.dtype), vh)

    out = lax.map(block, (qb, jnp.arange(nb)))
    return jnp.moveaxis(out, 0, 1).reshape(b, L, SB_WIDTH).astype(q.dtype)


def hybrid_layer(h_res, valid, w_in, conv_a, ssd_conv_w, ssd_conv_b, ssd_dt_bias, ssd_a_log,
                 ssd_d, ssd_norm, w_branch, w_out, w_ffn_in, w_ffn_out,
                 n_mix_pre, n_mix_post, n_ffn_pre, n_ffn_post):
    b, L, _ = h_res.shape
    h = rmsnorm(h_res, n_mix_pre)
    proj = jnp.einsum('bld,de->ble', h, w_in)
    (a_b, a_c, a_x, s_z, s_xbc, s_dt, r_q, r_k, r_v, r_g,
     sb_q, sb_k, sb_v, gate_logits) = jnp.split(proj, IN_SPLITS, axis=-1)
    y_a = short_conv_mixer(a_b, a_c, a_x, conv_a, valid)
    y_b = ssd_mixer(s_z, s_xbc, s_dt, ssd_conv_w, ssd_conv_b, ssd_dt_bias, ssd_a_log, ssd_d, ssd_norm, valid)
    y_c = retention_mixer(r_q, r_k, r_v, r_g, valid)
    y_d = stick_breaking_mixer(sb_q, sb_k, sb_v, valid)
    branches = jnp.stack([y_a, y_b, y_c, y_d], axis=2).astype(h.dtype)
    up = jnp.einsum('blnw,nwd->blnd', branches, w_branch)
    gates = jax.nn.sigmoid(gate_logits.reshape(b, L, N_BRANCH, D_MODEL))
    merged = jnp.sum(gates * up, axis=2)
    mix = jnp.einsum('bld,de->ble', merged, w_out)
    h_res = h_res + rmsnorm(mix, n_mix_post)
    f = jnp.einsum('bld,df->blf', rmsnorm(h_res, n_ffn_pre), w_ffn_in)
    f_gate, f_up = jnp.split(f, 2, axis=-1)
    f = jnp.einsum('blf,fd->bld', jax.nn.silu(f_gate) * f_up, w_ffn_out)
    return h_res + rmsnorm(f, n_ffn_post)


def setup_inputs(seed: int = 0) -> dict:
    key = jax.random.key(seed)
    ks = jax.random.split(key, 18)
    f32 = jnp.float32

    def nrm(k, shape, scale):
        return jax.random.normal(k, shape, f32) * scale

    dt0 = jnp.exp(jax.random.uniform(ks[6], (DEPTH, SSD_HEADS), f32, math.log(1e-3), math.log(1e-1)))
    return {
        'x': nrm(ks[0], (BATCH, SEQ, D_MODEL), 1.0),
        'meta': nrm(ks[1], (N_META, D_MODEL), 1.0),
        'w_in': nrm(ks[2], (DEPTH, D_MODEL, IN_WIDTH), D_MODEL ** -0.5),
        'conv_a': nrm(ks[3], (DEPTH, CONV_A_K, CONV_A_WIDTH), CONV_A_K ** -0.5),
        'ssd_conv_w': nrm(ks[4], (DEPTH, SSD_CONV_K, SSD_CONV_DIM), SSD_CONV_K ** -0.5),
        'ssd_conv_b': nrm(ks[5], (DEPTH, SSD_CONV_DIM), 0.02),
        'ssd_dt_bias': dt0 + jnp.log(-jnp.expm1(-dt0)),
        'ssd_a_log': jnp.log(jax.random.uniform(ks[7], (DEPTH, SSD_HEADS), f32, 1.0, 16.0)),
        'ssd_d': 1.0 + nrm(ks[8], (DEPTH, SSD_HEADS), 0.1),
        'ssd_norm': 1.0 + nrm(ks[9], (DEPTH, SSD_INNER), 0.02),
        'w_branch': nrm(ks[10], (DEPTH, N_BRANCH, BRANCH_WIDTH, D_MODEL), BRANCH_WIDTH ** -0.5),
        'w_out': nrm(ks[11], (DEPTH, D_MODEL, D_MODEL), D_MODEL ** -0.5),
        'w_ffn_in': nrm(ks[12], (DEPTH, D_MODEL, 2 * D_FF), D_MODEL ** -0.5),
        'w_ffn_out': nrm(ks[13], (DEPTH, D_FF, D_MODEL), D_FF ** -0.5),
        'norm_mix_pre': 1.0 + nrm(ks[14], (DEPTH, D_MODEL), 0.02),
        'norm_mix_post': 1.0 + nrm(ks[15], (DEPTH, D_MODEL), 0.02),
        'norm_ffn_pre': 1.0 + nrm(ks[16], (DEPTH, D_MODEL), 0.02),
        'norm_ffn_post': 1.0 + nrm(ks[17], (DEPTH, D_MODEL), 0.02),
    }


def reference(x, meta, w_in, conv_a, ssd_conv_w, ssd_conv_b, ssd_dt_bias, ssd_a_log, ssd_d,
              ssd_norm, w_branch, w_out, w_ffn_in, w_ffn_out,
              norm_mix_pre, norm_mix_post, norm_ffn_pre, norm_ffn_post):
    b = x.shape[0]
    dtype = x.dtype
    h = jnp.concatenate([
        jnp.zeros((b, PAD, D_MODEL), dtype),
        jnp.broadcast_to(meta.astype(dtype)[None], (b, N_META, D_MODEL)),
        x,
    ], axis=1)
    valid = jnp.arange(h.shape[1]) >= PAD
    for l in range(DEPTH):
        h = hybrid_layer(h, valid, w_in[l], conv_a[l], ssd_conv_w[l], ssd_conv_b[l],
                         ssd_dt_bias[l], ssd_a_log[l], ssd_d[l], ssd_norm[l], w_branch[l],
                         w_out[l], w_ffn_in[l], w_ffn_out[l], norm_mix_pre[l],
                         norm_mix_post[l], norm_ffn_pre[l], norm_ffn_post[l])
    return h[:, CHUNK:]
```

```python
import functools
import math

import numpy as np
import jax
import jax.numpy as jnp
from jax import lax
from jax.experimental import pallas as pl
from jax.experimental.pallas import tpu as pltpu

F32 = jnp.float32
BF16 = jnp.bfloat16

D_MODEL = 1024
N_META = 16
CHUNK = 128
PAD = CHUNK - N_META
EPS = 1e-6

SSD_HEAD_DIM = 64
SSD_HEADS = 16
SSD_INNER = 1024
SSD_GROUPS = 4
SSD_STATE = 128
SSD_CONV_K = 4
SSD_CONV_DIM = SSD_INNER + 2 * SSD_GROUPS * SSD_STATE
SSD_GROUP_WIDTH = SSD_INNER // SSD_GROUPS
CONV_A_K = 3
RET_HEADS = 4
RET_QK_DIM = 256
RET_V_DIM = 256
ROPE_BASE = 10000.0
SB_HEADS = 8
SB_HEAD_DIM = 128
N_BRANCH = 4
D_FF = 2816

COL_CONV = 0
COL_SSD_Z = 3072
COL_SSD_XBC = 4096
COL_RET_QK = 6144
COL_RET_VG = 8192
COL_SB_Q = 10240
COL_SB_K = 11264
COL_SB_V = 12288
COL_GATE = 13312
PROJ_WIDTH = 17408
DT_COL_START = 6144
DT_LANES = 128

HIST = 8
SB_UNDERFLOW = 104.0
VMEM_LIMIT = 56 * 1024 * 1024


def _rmsnorm(x, w):
    return x * lax.rsqrt(jnp.mean(x * x, axis=-1, keepdims=True) + EPS) * w


def _sigmoid(x):
    return 1.0 / (1.0 + jnp.exp(-x))


def _silu(x):
    return x * _sigmoid(x)


def _softplus(x):
    return jnp.maximum(x, 0.0) + jnp.log1p(jnp.exp(-jnp.abs(x)))


def _split3(x):
    hi = x.astype(BF16)
    r = x - hi.astype(F32)
    mid = r.astype(BF16)
    lo = (r - mid.astype(F32)).astype(BF16)
    return hi, mid, lo


def _dot(a, b):
    return jnp.dot(a, b, preferred_element_type=F32)


def _dot_nt(a, b):
    return lax.dot_general(a, b, (((1,), (1,)), ((), ())), preferred_element_type=F32)


def _dot_tn(a, b):
    return lax.dot_general(a, b, (((0,), (0,)), ((), ())), preferred_element_type=F32)


def _row_tile(n_rows, target):
    best = 16
    for t in range(16, min(n_rows, target) + 1, 16):
        if n_rows % t == 0:
            best = t
    return best


def _inproj_kernel(x_ref, nw_ref, w_ref, wdt_ref, o_ref, dt_ref, xn_ref):
    @pl.when(pl.program_id(1) == 0)
    def _():
        xn = _rmsnorm(x_ref[...], nw_ref[...]).astype(BF16)
        xn_ref[...] = xn
        dt_ref[...] = _dot(xn, wdt_ref[...])

    o_ref[...] = _dot(xn_ref[...], w_ref[...]).astype(o_ref.dtype)


def _inproj(h, norm_w, w_main, w_dt):
    n_rows = h.shape[0]
    tm = _row_tile(n_rows, 1376)
    tn = 1024
    return pl.pallas_call(
        _inproj_kernel,
        grid=(n_rows // tm, PROJ_WIDTH // tn),
        in_specs=[
            pl.BlockSpec((tm, D_MODEL), lambda i, j: (i, 0)),
            pl.BlockSpec((1, D_MODEL), lambda i, j: (0, 0)),
            pl.BlockSpec((D_MODEL, tn), lambda i, j: (0, j)),
            pl.BlockSpec((D_MODEL, DT_LANES), lambda i, j: (0, 0)),
        ],
        out_specs=[
            pl.BlockSpec((tm, tn), lambda i, j: (i, j)),
            pl.BlockSpec((tm, DT_LANES), lambda i, j: (i, 0)),
        ],
        out_shape=[
            jax.ShapeDtypeStruct((n_rows, PROJ_WIDTH), BF16),
            jax.ShapeDtypeStruct((n_rows, DT_LANES), F32),
        ],
        scratch_shapes=[pltpu.VMEM((tm, D_MODEL), BF16)],
        compiler_params=pltpu.CompilerParams(
            dimension_semantics=("parallel", "arbitrary"), vmem_limit_bytes=VMEM_LIMIT),
    )(h, norm_w, w_main, w_dt)


def _mixer_kernel(conv_ref, z_ref, xbc_ref, qk_ref, vg_ref, dt_ref, cos_ref, sin_ref,
                  conva_ref, sconvw_ref, sconvb_ref, dtb_ref, alog_ref, dskip_ref, snorm_ref,
                  expand_ref,
                  ya_ref, yb_ref, yc_ref,
                  uhist_ref, xhist_ref, sstate_ref, rstate_ref):
    c = pl.program_id(0)

    @pl.when(c == 0)
    def _():
        uhist_ref[0:HIST, :] = jnp.zeros((HIST, D_MODEL), F32)
        xhist_ref[0:HIST, :] = jnp.zeros((HIST, SSD_CONV_DIM), F32)
        sstate_ref[...] = jnp.zeros_like(sstate_ref)
        rstate_ref[...] = jnp.zeros_like(rstate_ref)

    def valid(shape):
        return (lax.broadcasted_iota(jnp.int32, shape, 0) + c * CHUNK) >= PAD

    li = lax.broadcasted_iota(jnp.int32, (CHUNK, CHUNK), 0)
    si = lax.broadcasted_iota(jnp.int32, (CHUNK, CHUNK), 1)
    causal = li >= si

    b_gate = conv_ref[:, 0:D_MODEL].astype(F32)
    c_gate = conv_ref[:, D_MODEL:2 * D_MODEL].astype(F32)
    xa = conv_ref[:, 2 * D_MODEL:3 * D_MODEL].astype(F32)
    u = jnp.where(valid((CHUNK, D_MODEL)), c_gate * xa, 0.0)
    uhist_ref[HIST:HIST + CHUNK, :] = u
    conv = u * conva_ref[CONV_A_K - 1:CONV_A_K, :]
    for i in range(CONV_A_K - 1):
        off = HIST - (CONV_A_K - 1) + i
        conv = conv + uhist_ref[off:off + CHUNK, :] * conva_ref[i:i + 1, :]
    ya_ref[...] = (b_gate * conv).astype(ya_ref.dtype)
    uhist_ref[0:HIST, :] = uhist_ref[CHUNK:CHUNK + HIST, :]

    xin = jnp.where(valid((CHUNK, SSD_CONV_DIM)), xbc_ref[...].astype(F32), 0.0)
    xhist_ref[HIST:HIST + CHUNK, :] = xin
    xc = xin * sconvw_ref[SSD_CONV_K - 1:SSD_CONV_K, :] + sconvb_ref[...]
    for i in range(SSD_CONV_K - 1):
        off = HIST - (SSD_CONV_K - 1) + i
        xc = xc + xhist_ref[off:off + CHUNK, :] * sconvw_ref[i:i + 1, :]
    xhist_ref[0:HIST, :] = xhist_ref[CHUNK:CHUNK + HIST, :]
    xc = _silu(xc)
    xs = jnp.where(valid((CHUNK, SSD_INNER)), xc[:, 0:SSD_INNER], 0.0)
    bm = xc[:, SSD_INNER:SSD_INNER + SSD_GROUPS * SSD_STATE].astype(BF16)
    cm = xc[:, SSD_INNER + SSD_GROUPS * SSD_STATE:SSD_CONV_DIM].astype(BF16)

    dtv = _softplus(dt_ref[...] + dtb_ref[...])
    a = -jnp.exp(alog_ref[...]) * dtv
    tri = causal.astype(BF16)
    acs = sum(_dot(tri, part) for part in _split3(a))
    acs_t = acs.T
    acs_last = acs[CHUNK - 1:CHUNK, :]
    expand = expand_ref[...]

    def expand_heads(v):
        return sum(_dot(part, expand) for part in _split3(v))

    dt_x = expand_heads(dtv)
    eacs_x = jnp.exp(expand_heads(acs))
    dstate_x = jnp.exp(expand_heads(acs_last - acs))
    cdecay_x = eacs_x[CHUNK - 1:CHUNK, :]
    xdt = xs * dt_x
    xdt_b = xdt.astype(BF16)
    xw_b = (xdt * dstate_x).astype(BF16)

    y_groups = []
    hpg = SSD_HEADS // SSD_GROUPS
    for g in range(SSD_GROUPS):
        gs = slice(g * SSD_GROUP_WIDTH, (g + 1) * SSD_GROUP_WIDTH)
        cc = cm[:, g * SSD_STATE:(g + 1) * SSD_STATE]
        bc = bm[:, g * SSD_STATE:(g + 1) * SSD_STATE]
        cb = _dot_nt(cc, bc)
        yd = []
        for j in range(hpg):
            h = g * hpg + j
            diff = acs[:, h:h + 1] - acs_t[h:h + 1, :]
            seg = jnp.exp(jnp.where(causal, diff, -jnp.inf))
            hs = slice(h * SSD_HEAD_DIM, (h + 1) * SSD_HEAD_DIM)
            yd.append(_dot((cb * seg).astype(BF16), xdt_b[:, hs]))
        y_diag = jnp.concatenate(yd, axis=-1)
        prev = sstate_ref[g]
        y_off = _dot(cc, prev.astype(BF16)) * eacs_x[:, gs]
        sstate_ref[g] = prev * cdecay_x[:, gs] + _dot_tn(bc, xw_b[:, gs])
        y_groups.append(y_diag + y_off)
    y = jnp.concatenate(y_groups, axis=-1) + xs * dskip_ref[...]
    y = y * _silu(z_ref[...].astype(F32))
    yn = []
    for g in range(SSD_GROUPS):
        yg = y[:, g * SSD_GROUP_WIDTH:(g + 1) * SSD_GROUP_WIDTH]
        yn.append(yg * lax.rsqrt(jnp.mean(yg * yg, axis=-1, keepdims=True) + EPS))
    yb_ref[...] = (jnp.concatenate(yn, axis=-1) * snorm_ref[...]).astype(yb_ref.dtype)

    cos = cos_ref[...]
    sin = sin_ref[...]
    half = RET_QK_DIM // 2
    rel = (li - si).astype(F32)
    row_w = lax.broadcasted_iota(jnp.int32, (CHUNK, RET_QK_DIM), 0).astype(F32)
    v_valid = valid((CHUNK, RET_V_DIM))
    yc = []
    for h in range(RET_HEADS):
        log_gamma = math.log(1.0 - 2.0 ** (-5.0 - h))
        q1 = qk_ref[:, h * RET_QK_DIM:h * RET_QK_DIM + half].astype(F32)
        q2 = qk_ref[:, h * RET_QK_DIM + half:(h + 1) * RET_QK_DIM].astype(F32)
        ko = RET_HEADS * RET_QK_DIM
        k1 = qk_ref[:, ko + h * RET_QK_DIM:ko + h * RET_QK_DIM + half].astype(F32)
        k2 = qk_ref[:, ko + h * RET_QK_DIM + half:ko + (h + 1) * RET_QK_DIM].astype(F32)
        qr = jnp.concatenate([q1 * cos - q2 * sin, q1 * sin + q2 * cos], axis=-1)
        kr = jnp.concatenate([k1 * cos - k2 * sin, k1 * sin + k2 * cos], axis=-1) * (RET_QK_DIM ** -0.5)
        vr = jnp.where(v_valid, vg_ref[:, h * RET_V_DIM:(h + 1) * RET_V_DIM].astype(F32), 0.0).astype(BF16)
        gate = vg_ref[:, RET_HEADS * RET_V_DIM + h * RET_V_DIM:
                      RET_HEADS * RET_V_DIM + (h + 1) * RET_V_DIM].astype(F32)
        qr_b = qr.astype(BF16)
        dmask = jnp.where(causal, jnp.exp(log_gamma * jnp.maximum(rel, 0.0)), 0.0)
        scores = _dot_nt(qr_b, kr.astype(BF16)) * dmask
        y_in = _dot(scores.astype(BF16), vr)
        k_decay = jnp.exp(log_gamma * (CHUNK - 1.0 - row_w))
        kv = _dot_tn((kr * k_decay).astype(BF16), vr)
        prev = rstate_ref[h]
        q_decay = jnp.exp(log_gamma * (row_w + 1.0))
        y_cr = _dot(qr_b, prev.astype(BF16)) * q_decay
        rstate_ref[h] = prev * math.exp(log_gamma * CHUNK) + kv
        yh = y_in + y_cr
        mu = jnp.mean(yh, axis=-1, keepdims=True)
        yh = yh - mu
        var = jnp.mean(yh * yh, axis=-1, keepdims=True)
        yc.append(yh * lax.rsqrt(var + EPS) * _silu(gate))
    yc_ref[...] = jnp.concatenate(yc, axis=-1).astype(yc_ref.dtype)


def _mixers(proj, dt_raw, cos, sin, conv_a, sconv_w, sconv_b, dt_bias, a_log, d_skip, snorm, expand):
    n_rows = proj.shape[0]
    nc = n_rows // CHUNK

    def col(width, start):
        assert start % width == 0
        return pl.BlockSpec((CHUNK, width), lambda c, s=start // width: (c, s))

    def full(arr):
        return pl.BlockSpec(arr.shape, lambda c: (0,) * arr.ndim)

    row_block = pl.BlockSpec((CHUNK, D_MODEL), lambda c: (c, 0))
    lane_block = pl.BlockSpec((CHUNK, DT_LANES), lambda c: (c, 0))
    weights = (conv_a, sconv_w, sconv_b, dt_bias, a_log, d_skip, snorm, expand)
    return pl.pallas_call(
        _mixer_kernel,
        grid=(nc,),
        in_specs=[
            col(3 * D_MODEL, COL_CONV), col(D_MODEL, COL_SSD_Z), col(SSD_CONV_DIM, COL_SSD_XBC),
            col(2048, COL_RET_QK), col(2048, COL_RET_VG),
            lane_block, lane_block, lane_block,
        ] + [full(w) for w in weights],
        out_specs=[row_block, row_block, row_block],
        out_shape=[jax.ShapeDtypeStruct((n_rows, D_MODEL), BF16)] * 3,
        scratch_shapes=[
            pltpu.VMEM((HIST + CHUNK, D_MODEL), F32),
            pltpu.VMEM((HIST + CHUNK, SSD_CONV_DIM), F32),
            pltpu.VMEM((SSD_GROUPS, SSD_STATE, SSD_GROUP_WIDTH), F32),
            pltpu.VMEM((RET_HEADS, RET_QK_DIM, RET_V_DIM), F32),
        ],
        compiler_params=pltpu.CompilerParams(
            dimension_semantics=("arbitrary",), vmem_limit_bytes=VMEM_LIMIT),
    )(proj, proj, proj, proj, proj, dt_raw, cos, sin, *weights)


def _sb_kernel(q_ref, k_ref, v_ref, o_ref, acc_ref, run_ref):
    i = pl.program_id(1)
    q = q_ref[...]
    acc_ref[...] = jnp.zeros_like(acc_ref)
    run_ref[...] = jnp.zeros_like(run_ref)
    rows = lax.broadcasted_iota(jnp.int32, (CHUNK, CHUNK), 0)
    cols = lax.broadcasted_iota(jnp.int32, (CHUNK, CHUNK), 1)
    suffix = (rows >= cols).astype(BF16)
    scale = SB_HEAD_DIM ** -0.5

    def body(carry):
        j, _ = carry
        start = pl.multiple_of(j * CHUNK, CHUNK)
        k = k_ref[pl.ds(start, CHUNK), :]
        v = v_ref[pl.ds(start, CHUNK), :]
        z = _dot_nt(q, k) * scale
        key_pos = cols + j * CHUNK
        mask = (key_pos < rows + i * CHUNK) & (key_pos >= PAD)
        sp = jnp.where(mask, _softplus(z), 0.0)
        sp_hi = sp.astype(BF16)
        sp_lo = (sp - sp_hi.astype(F32)).astype(BF16)
        csum = _dot(sp_hi, suffix) + _dot(sp_lo, suffix)
        run = run_ref[...]
        w = jnp.where(mask, jnp.exp(z - csum - run), 0.0)
        acc_ref[...] += _dot(w.astype(BF16), v)
        run = run + csum[:, 0:1]
        run_ref[...] = run
        return j - 1, jnp.min(run)

    def cond(carry):
        j, run_min = carry
        return (j >= 0) & (run_min < SB_UNDERFLOW)

    lax.while_loop(cond, body, (i, jnp.float32(0.0)))
    o_ref[...] = acc_ref[...].astype(o_ref.dtype)


def _stick_breaking(proj):
    n_rows = proj.shape[0]
    nq = n_rows // CHUNK

    def head_cols(start):
        return lambda h, i, s=start // SB_HEAD_DIM: (0, s + h)

    return pl.pallas_call(
        _sb_kernel,
        grid=(SB_HEADS, nq),
        in_specs=[
            pl.BlockSpec((CHUNK, SB_HEAD_DIM), lambda h, i, s=COL_SB_Q // SB_HEAD_DIM: (i, s + h)),
            pl.BlockSpec((n_rows, SB_HEAD_DIM), head_cols(COL_SB_K)),
            pl.BlockSpec((n_rows, SB_HEAD_DIM), head_cols(COL_SB_V)),
        ],
        out_specs=pl.BlockSpec((CHUNK, SB_HEAD_DIM), lambda h, i: (i, h)),
        out_shape=jax.ShapeDtypeStruct((n_rows, SB_HEADS * SB_HEAD_DIM), BF16),
        scratch_shapes=[
            pltpu.VMEM((CHUNK, SB_HEAD_DIM), F32),
            pltpu.VMEM((CHUNK, 1), F32),
        ],
        compiler_params=pltpu.CompilerParams(
            dimension_semantics=("arbitrary", "arbitrary"), vmem_limit_bytes=VMEM_LIMIT),
    )(proj, proj, proj)


def _merge_kernel(ya_ref, yb_ref, yc_ref, yd_ref, ga_ref, gb_ref, gc_ref, gd_ref, h_ref,
                  wb_ref, wo_ref, nw_ref, o_ref):
    merged = None
    branches = ((ya_ref, ga_ref), (yb_ref, gb_ref), (yc_ref, gc_ref), (yd_ref, gd_ref))
    for n, (y_ref, g_ref) in enumerate(branches):
        up = _dot(y_ref[...], wb_ref[n])
        gate = _sigmoid(g_ref[...].astype(F32))
        merged = gate * up if merged is None else merged + gate * up
    mix = _dot(merged.astype(BF16), wo_ref[...])
    o_ref[...] = h_ref[...] + _rmsnorm(mix, nw_ref[...])


def _merge(ya, yb, yc, yd, proj, h, w_branch, w_out, norm_w):
    n_rows = h.shape[0]
    tm = _row_tile(n_rows, 384)
    row_block = pl.BlockSpec((tm, D_MODEL), lambda i: (i, 0))
    return pl.pallas_call(
        _merge_kernel,
        grid=(n_rows // tm,),
        in_specs=[
            row_block, row_block, row_block, row_block,
        ] + [
            pl.BlockSpec((tm, D_MODEL), lambda i, s=COL_GATE // D_MODEL + n: (i, s)) for n in range(N_BRANCH)
        ] + [
            row_block,
            pl.BlockSpec((N_BRANCH, D_MODEL, D_MODEL), lambda i: (0, 0, 0)),
            pl.BlockSpec((D_MODEL, D_MODEL), lambda i: (0, 0)),
            pl.BlockSpec((1, D_MODEL), lambda i: (0, 0)),
        ],
        out_specs=row_block,
        out_shape=jax.ShapeDtypeStruct((n_rows, D_MODEL), F32),
        compiler_params=pltpu.CompilerParams(
            dimension_semantics=("parallel",), vmem_limit_bytes=VMEM_LIMIT),
    )(ya, yb, yc, yd, proj, proj, proj, proj, h, w_branch, w_out, norm_w)


def _ffn_kernel(h_ref, w1_ref, w2_ref, n1_ref, n2_ref, o_ref):
    h = h_ref[...]
    xn = _rmsnorm(h, n1_ref[...]).astype(BF16)
    f = _dot(xn, w1_ref[...])
    act = (_silu(f[:, 0:D_FF]) * f[:, D_FF:2 * D_FF]).astype(BF16)
    o_ref[...] = h + _rmsnorm(_dot(act, w2_ref[...]), n2_ref[...])


def _ffn(h, w1, w2, n1, n2):
    n_rows = h.shape[0]
    tm = _row_tile(n_rows, 384)
    row_block = pl.BlockSpec((tm, D_MODEL), lambda i: (i, 0))
    return pl.pallas_call(
        _ffn_kernel,
        grid=(n_rows // tm,),
        in_specs=[
            row_block,
            pl.BlockSpec((D_MODEL, 2 * D_FF), lambda i: (0, 0)),
            pl.BlockSpec((D_FF, D_MODEL), lambda i: (0, 0)),
            pl.BlockSpec((1, D_MODEL), lambda i: (0, 0)),
            pl.BlockSpec((1, D_MODEL), lambda i: (0, 0)),
        ],
        out_specs=row_block,
        out_shape=jax.ShapeDtypeStruct((n_rows, D_MODEL), F32),
        compiler_params=pltpu.CompilerParams(
            dimension_semantics=("parallel",), vmem_limit_bytes=VMEM_LIMIT),
    )(h, w1, w2, n1, n2)


def _rope_tables(n_rows):
    half = RET_QK_DIM // 2
    inv = ROPE_BASE ** (-jnp.arange(half, dtype=F32) / half)
    ang = jnp.arange(n_rows).astype(F32)[:, None] * inv[None, :]
    return jnp.cos(ang), jnp.sin(ang)


def _head_expand_matrix():
    e = np.zeros((DT_LANES, SSD_INNER), np.float32)
    for h in range(SSD_HEADS):
        e[h, h * SSD_HEAD_DIM:(h + 1) * SSD_HEAD_DIM] = 1.0
    return jnp.asarray(e, BF16)


def _pad_lanes(v):
    return jnp.pad(v.astype(F32), (0, DT_LANES - v.shape[0]))[None, :]


def kernel(x, meta, w_in, conv_a, ssd_conv_w, ssd_conv_b, ssd_dt_bias, ssd_a_log, ssd_d, ssd_norm,
           w_branch, w_out, w_ffn_in, w_ffn_out, norm_mix_pre, norm_mix_post, norm_ffn_pre,
           norm_ffn_post):
    batch, seq, _ = x.shape
    assert batch == 1 and seq % CHUNK == 0
    depth = w_in.shape[0]
    n_rows = seq + CHUNK
    h = jnp.concatenate([jnp.zeros((PAD, D_MODEL), F32), meta.astype(F32), x[0].astype(F32)], axis=0)
    cos, sin = _rope_tables(n_rows)
    expand = _head_expand_matrix()
    dt_end = DT_COL_START + SSD_HEADS
    for l in range(depth):
        w_main = jnp.concatenate([w_in[l, :, :DT_COL_START], w_in[l, :, dt_end:]], axis=1).astype(BF16)
        w_dt = jnp.pad(w_in[l, :, DT_COL_START:dt_end], ((0, 0), (0, DT_LANES - SSD_HEADS))).astype(BF16)
        proj, dt_raw = _inproj(h, norm_mix_pre[l][None, :], w_main, w_dt)
        ya, yb, yc = _mixers(
            proj, dt_raw, cos, sin, conv_a[l], ssd_conv_w[l], ssd_conv_b[l][None, :],
            _pad_lanes(ssd_dt_bias[l]), _pad_lanes(ssd_a_log[l]),
            jnp.repeat(ssd_d[l].astype(F32), SSD_HEAD_DIM)[None, :], ssd_norm[l][None, :], expand)
        yd = _stick_breaking(proj)
        h = _merge(ya, yb, yc, yd, proj, h, w_branch[l].astype(BF16), w_out[l].astype(BF16),
                   norm_mix_post[l][None, :])
        h = _ffn(h, w_ffn_in[l].astype(BF16), w_ffn_out[l].astype(BF16),
                 norm_ffn_pre[l][None, :], norm_ffn_post[l][None, :])
    return h[CHUNK:][None].astype(x.dtype)
```

```python
import functools
import math

import numpy as np
import jax
import jax.numpy as jnp
from jax import lax
from jax.experimental import pallas as pl
from jax.experimental.pallas import tpu as pltpu

F32 = jnp.float32
BF16 = jnp.bfloat16

D_MODEL = 1024
N_META = 16
CHUNK = 128
PAD = CHUNK - N_META
EPS = 1e-6

SSD_HEAD_DIM = 64
SSD_HEADS = 16
SSD_INNER = 1024
SSD_GROUPS = 4
SSD_STATE = 128
SSD_CONV_K = 4
SSD_CONV_DIM = SSD_INNER + 2 * SSD_GROUPS * SSD_STATE
SSD_GROUP_WIDTH = SSD_INNER // SSD_GROUPS
CONV_A_K = 3
RET_HEADS = 4
RET_QK_DIM = 256
RET_V_DIM = 256
ROPE_BASE = 10000.0
SB_HEADS = 8
SB_HEAD_DIM = 128
N_BRANCH = 4
D_FF = 2816

COL_CONV = 0
COL_SSD_Z = 3072
COL_SSD_XBC = 4096
COL_RET_QK = 6144
COL_RET_VG = 8192
COL_SB_Q = 10240
COL_SB_K = 11264
COL_SB_V = 12288
COL_GATE = 13312
PROJ_WIDTH = 17408
DT_COL_START = 6144
DT_LANES = 128

HIST = 8
SB_UNDERFLOW = 104.0
SB_RING = 4
VMEM_LIMIT = 56 * 1024 * 1024


def _rmsnorm(x, w):
    return x * lax.rsqrt(jnp.mean(x * x, axis=-1, keepdims=True) + EPS) * w


def _sigmoid(x):
    return 1.0 / (1.0 + jnp.exp(-x))


def _silu(x):
    return x * _sigmoid(x)


def _softplus(x):
    return jnp.maximum(x, 0.0) + jnp.log1p(jnp.exp(-jnp.abs(x)))


def _split3(x):
    hi = x.astype(BF16)
    r = x - hi.astype(F32)
    mid = r.astype(BF16)
    lo = (r - mid.astype(F32)).astype(BF16)
    return hi, mid, lo


def _dot(a, b):
    return jnp.dot(a, b, preferred_element_type=F32)


def _dot_nt(a, b):
    return lax.dot_general(a, b, (((1,), (1,)), ((), ())), preferred_element_type=F32)


def _dot_tn(a, b):
    return lax.dot_general(a, b, (((0,), (0,)), ((), ())), preferred_element_type=F32)


def _row_tile(n_rows, target):
    best = 16
    for t in range(16, min(n_rows, target) + 1, 16):
        if n_rows % t == 0:
            best = t
    return best


def _inproj_kernel(x_ref, nw_ref, w_ref, wdt_ref, o_ref, dt_ref, xn_ref):
    @pl.when(pl.program_id(1) == 0)
    def _():
        xn = _rmsnorm(x_ref[...], nw_ref[...]).astype(BF16)
        xn_ref[...] = xn
        dt_ref[...] = _dot(xn, wdt_ref[...])

    o_ref[...] = _dot(xn_ref[...], w_ref[...]).astype(o_ref.dtype)


def _inproj(h, norm_w, w_main, w_dt):
    n_rows = h.shape[0]
    tm = _row_tile(n_rows, 1376)
    tn = 1024
    return pl.pallas_call(
        _inproj_kernel,
        grid=(n_rows // tm, PROJ_WIDTH // tn),
        in_specs=[
            pl.BlockSpec((tm, D_MODEL), lambda i, j: (i, 0)),
            pl.BlockSpec((1, D_MODEL), lambda i, j: (0, 0)),
            pl.BlockSpec((D_MODEL, tn), lambda i, j: (0, j)),
            pl.BlockSpec((D_MODEL, DT_LANES), lambda i, j: (0, 0)),
        ],
        out_specs=[
            pl.BlockSpec((tm, tn), lambda i, j: (i, j)),
            pl.BlockSpec((tm, DT_LANES), lambda i, j: (i, 0)),
        ],
        out_shape=[
            jax.ShapeDtypeStruct((n_rows, PROJ_WIDTH), BF16),
            jax.ShapeDtypeStruct((n_rows, DT_LANES), F32),
        ],
        scratch_shapes=[pltpu.VMEM((tm, D_MODEL), BF16)],
        compiler_params=pltpu.CompilerParams(
            dimension_semantics=("parallel", "arbitrary"), vmem_limit_bytes=VMEM_LIMIT),
    )(h, norm_w, w_main, w_dt)


def _mixer_kernel(conv_ref, z_ref, xbc_ref, qk_ref, vg_ref, dt_ref, cos_ref, sin_ref,
                  conva_ref, sconvw_ref, sconvb_ref, dtb_ref, alog_ref, dskip_ref, snorm_ref,
                  expand_ref,
                  ya_ref, yb_ref, yc_ref,
                  uhist_ref, xhist_ref, sstate_ref, rstate_ref):
    c = pl.program_id(0)

    @pl.when(c == 0)
    def _():
        uhist_ref[0:HIST, :] = jnp.zeros((HIST, D_MODEL), F32)
        xhist_ref[0:HIST, :] = jnp.zeros((HIST, SSD_CONV_DIM), F32)
        sstate_ref[...] = jnp.zeros_like(sstate_ref)
        rstate_ref[...] = jnp.zeros_like(rstate_ref)

    def valid(shape):
        return (lax.broadcasted_iota(jnp.int32, shape, 0) + c * CHUNK) >= PAD

    li = lax.broadcasted_iota(jnp.int32, (CHUNK, CHUNK), 0)
    si = lax.broadcasted_iota(jnp.int32, (CHUNK, CHUNK), 1)
    causal = li >= si

    b_gate = conv_ref[:, 0:D_MODEL].astype(F32)
    c_gate = conv_ref[:, D_MODEL:2 * D_MODEL].astype(F32)
    xa = conv_ref[:, 2 * D_MODEL:3 * D_MODEL].astype(F32)
    u = jnp.where(valid((CHUNK, D_MODEL)), c_gate * xa, 0.0)
    uhist_ref[HIST:HIST + CHUNK, :] = u
    conv = u * conva_ref[CONV_A_K - 1:CONV_A_K, :]
    for i in range(CONV_A_K - 1):
        off = HIST - (CONV_A_K - 1) + i
        conv = conv + uhist_ref[off:off + CHUNK, :] * conva_ref[i:i + 1, :]
    ya_ref[...] = (b_gate * conv).astype(ya_ref.dtype)
    uhist_ref[0:HIST, :] = uhist_ref[CHUNK:CHUNK + HIST, :]

    xin = jnp.where(valid((CHUNK, SSD_CONV_DIM)), xbc_ref[...].astype(F32), 0.0)
    xhist_ref[HIST:HIST + CHUNK, :] = xin
    xc = xin * sconvw_ref[SSD_CONV_K - 1:SSD_CONV_K, :] + sconvb_ref[...]
    for i in range(SSD_CONV_K - 1):
        off = HIST - (SSD_CONV_K - 1) + i
        xc = xc + xhist_ref[off:off + CHUNK, :] * sconvw_ref[i:i + 1, :]
    xhist_ref[0:HIST, :] = xhist_ref[CHUNK:CHUNK + HIST, :]
    xc = _silu(xc)
    xs = jnp.where(valid((CHUNK, SSD_INNER)), xc[:, 0:SSD_INNER], 0.0)
    bm = xc[:, SSD_INNER:SSD_INNER + SSD_GROUPS * SSD_STATE].astype(BF16)
    cm = xc[:, SSD_INNER + SSD_GROUPS * SSD_STATE:SSD_CONV_DIM].astype(BF16)

    dtv = _softplus(dt_ref[...] + dtb_ref[...])
    a = -jnp.exp(alog_ref[...]) * dtv
    tri = causal.astype(BF16)
    acs = sum(_dot(tri, part) for part in _split3(a))
    acs_t = acs.T
    acs_last = acs[CHUNK - 1:CHUNK, :]
    expand = expand_ref[...]

    def expand_heads(v):
        return sum(_dot(part, expand) for part in _split3(v))

    dt_x = expand_heads(dtv)
    eacs_x = jnp.exp(expand_heads(acs))
    dstate_x = jnp.exp(expand_heads(acs_last - acs))
    cdecay_x = eacs_x[CHUNK - 1:CHUNK, :]
    xdt = xs * dt_x
    xdt_b = xdt.astype(BF16)
    xw_b = (xdt * dstate_x).astype(BF16)

    y_groups = []
    hpg = SSD_HEADS // SSD_GROUPS
    for g in range(SSD_GROUPS):
        gs = slice(g * SSD_GROUP_WIDTH, (g + 1) * SSD_GROUP_WIDTH)
        cc = cm[:, g * SSD_STATE:(g + 1) * SSD_STATE]
        bc = bm[:, g * SSD_STATE:(g + 1) * SSD_STATE]
        cb = _dot_nt(cc, bc)
        yd = []
        for j in range(hpg):
            h = g * hpg + j
            diff = acs[:, h:h + 1] - acs_t[h:h + 1, :]
            seg = jnp.exp(jnp.where(causal, diff, -jnp.inf))
            hs = slice(h * SSD_HEAD_DIM, (h + 1) * SSD_HEAD_DIM)
            yd.append(_dot((cb * seg).astype(BF16), xdt_b[:, hs]))
        y_diag = jnp.concatenate(yd, axis=-1)
        prev = sstate_ref[g]
        y_off = _dot(cc, prev.astype(BF16)) * eacs_x[:, gs]
        sstate_ref[g] = prev * cdecay_x[:, gs] + _dot_tn(bc, xw_b[:, gs])
        y_groups.append(y_diag + y_off)
    y = jnp.concatenate(y_groups, axis=-1) + xs * dskip_ref[...]
    y = y * _silu(z_ref[...].astype(F32))
    yn = []
    for g in range(SSD_GROUPS):
        yg = y[:, g * SSD_GROUP_WIDTH:(g + 1) * SSD_GROUP_WIDTH]
        yn.append(yg * lax.rsqrt(jnp.mean(yg * yg, axis=-1, keepdims=True) + EPS))
    yb_ref[...] = (jnp.concatenate(yn, axis=-1) * snorm_ref[...]).astype(yb_ref.dtype)

    cos = cos_ref[...]
    sin = sin_ref[...]
    half = RET_QK_DIM // 2
    rel = (li - si).astype(F32)
    row_w = lax.broadcasted_iota(jnp.int32, (CHUNK, RET_QK_DIM), 0).astype(F32)
    v_valid = valid((CHUNK, RET_V_DIM))
    yc = []
    for h in range(RET_HEADS):
        log_gamma = math.log(1.0 - 2.0 ** (-5.0 - h))
        q1 = qk_ref[:, h * RET_QK_DIM:h * RET_QK_DIM + half].astype(F32)
        q2 = qk_ref[:, h * RET_QK_DIM + half:(h + 1) * RET_QK_DIM].astype(F32)
        ko = RET_HEADS * RET_QK_DIM
        k1 = qk_ref[:, ko + h * RET_QK_DIM:ko + h * RET_QK_DIM + half].astype(F32)
        k2 = qk_ref[:, ko + h * RET_QK_DIM + half:ko + (h + 1) * RET_QK_DIM].astype(F32)
        qr = jnp.concatenate([q1 * cos - q2 * sin, q1 * sin + q2 * cos], axis=-1)
        kr = jnp.concatenate([k1 * cos - k2 * sin, k1 * sin + k2 * cos], axis=-1) * (RET_QK_DIM ** -0.5)
        vr = jnp.where(v_valid, vg_ref[:, h * RET_V_DIM:(h + 1) * RET_V_DIM].astype(F32), 0.0).astype(BF16)
        gate = vg_ref[:, RET_HEADS * RET_V_DIM + h * RET_V_DIM:
                      RET_HEADS * RET_V_DIM + (h + 1) * RET_V_DIM].astype(F32)
        qr_b = qr.astype(BF16)
        dmask = jnp.where(causal, jnp.exp(log_gamma * jnp.maximum(rel, 0.0)), 0.0)
        scores = _dot_nt(qr_b, kr.astype(BF16)) * dmask
        y_in = _dot(scores.astype(BF16), vr)
        k_decay = jnp.exp(log_gamma * (CHUNK - 1.0 - row_w))
        kv = _dot_tn((kr * k_decay).astype(BF16), vr)
        prev = rstate_ref[h]
        q_decay = jnp.exp(log_gamma * (row_w + 1.0))
        y_cr = _dot(qr_b, prev.astype(BF16)) * q_decay
        rstate_ref[h] = prev * math.exp(log_gamma * CHUNK) + kv
        yh = y_in + y_cr
        mu = jnp.mean(yh, axis=-1, keepdims=True)
        yh = yh - mu
        var = jnp.mean(yh * yh, axis=-1, keepdims=True)
        yc.append(yh * lax.rsqrt(var + EPS) * _silu(gate))
    yc_ref[...] = jnp.concatenate(yc, axis=-1).astype(yc_ref.dtype)


def _mixers(proj, dt_raw, cos, sin, conv_a, sconv_w, sconv_b, dt_bias, a_log, d_skip, snorm, expand):
    n_rows = proj.shape[0]
    nc = n_rows // CHUNK

    def col(width, start):
        assert start % width == 0
        return pl.BlockSpec((CHUNK, width), lambda c, s=start // width: (c, s))

    def full(arr):
        return pl.BlockSpec(arr.shape, lambda c: (0,) * arr.ndim)

    row_block = pl.BlockSpec((CHUNK, D_MODEL), lambda c: (c, 0))
    lane_block = pl.BlockSpec((CHUNK, DT_LANES), lambda c: (c, 0))
    weights = (conv_a, sconv_w, sconv_b, dt_bias, a_log, d_skip, snorm, expand)
    return pl.pallas_call(
        _mixer_kernel,
        grid=(nc,),
        in_specs=[
            col(3 * D_MODEL, COL_CONV), col(D_MODEL, COL_SSD_Z), col(SSD_CONV_DIM, COL_SSD_XBC),
            col(2048, COL_RET_QK), col(2048, COL_RET_VG),
            lane_block, lane_block, lane_block,
        ] + [full(w) for w in weights],
        out_specs=[row_block, row_block, row_block],
        out_shape=[jax.ShapeDtypeStruct((n_rows, D_MODEL), BF16)] * 3,
        scratch_shapes=[
            pltpu.VMEM((HIST + CHUNK, D_MODEL), F32),
            pltpu.VMEM((HIST + CHUNK, SSD_CONV_DIM), F32),
            pltpu.VMEM((SSD_GROUPS, SSD_STATE, SSD_GROUP_WIDTH), F32),
            pltpu.VMEM((RET_HEADS, RET_QK_DIM, RET_V_DIM), F32),
        ],
        compiler_params=pltpu.CompilerParams(
            dimension_semantics=("arbitrary",), vmem_limit_bytes=VMEM_LIMIT),
    )(proj, proj, proj, proj, proj, dt_raw, cos, sin, *weights)


def _sb_kernel(q_ref, k_ref, v_ref, proj_hbm, o_ref, kbuf, vbuf, acc_ref, run_ref, sem):
    i = pl.program_id(0)
    kbuf[i % SB_RING] = k_ref[...]
    vbuf[i % SB_RING] = v_ref[...]
    acc_ref[...] = jnp.zeros_like(acc_ref)
    run_ref[...] = jnp.zeros_like(run_ref)
    rows = lax.broadcasted_iota(jnp.int32, (CHUNK, CHUNK), 0)
    cols = lax.broadcasted_iota(jnp.int32, (CHUNK, CHUNK), 1)
    suffix = (rows >= cols).astype(BF16)
    scale = SB_HEAD_DIM ** -0.5

    def fetch(j):
        far = (i - j) >= SB_RING

        @pl.when(far)
        def _():
            rows_j = pl.ds(pl.multiple_of(j * CHUNK, CHUNK), CHUNK)
            copies = [
                pltpu.make_async_copy(proj_hbm.at[rows_j, pl.ds(COL_SB_K, D_MODEL)], kbuf.at[SB_RING], sem.at[0]),
                pltpu.make_async_copy(proj_hbm.at[rows_j, pl.ds(COL_SB_V, D_MODEL)], vbuf.at[SB_RING], sem.at[1]),
            ]
            for cp in copies:
                cp.start()
            for cp in copies:
                cp.wait()

        return jnp.where(far, SB_RING, j % SB_RING)

    def visit(j, slot, causal, pad):
        mask = None
        if causal:
            mask = cols < rows
        if pad:
            in_seq = (cols + j * CHUNK) >= PAD
            mask = in_seq if mask is None else mask & in_seq
        heads = [slice(h * SB_HEAD_DIM, (h + 1) * SB_HEAD_DIM) for h in range(SB_HEADS)]
        zs = [_dot_nt(q_ref[:, hs], kbuf[slot, :, hs]) * scale for hs in heads]
        csums = []
        for z in zs:
            sp = jnp.maximum(z, 0.0) + jnp.log(1.0 + jnp.exp(-jnp.abs(z)))
            if mask is not None:
                sp = jnp.where(mask, sp, 0.0)
            sp_hi = sp.astype(BF16)
            sp_lo = (sp - sp_hi.astype(F32)).astype(BF16)
            csums.append(_dot(sp_hi, suffix) + _dot(sp_lo, suffix))
        run_min = None
        outs = []
        for h, (z, csum) in enumerate(zip(zs, csums)):
            run = run_ref[h]
            w = jnp.exp(z - csum - run)
            if mask is not None:
                w = jnp.where(mask, w, 0.0)
            outs.append(_dot(w.astype(BF16), vbuf[slot, :, heads[h]]))
            run = run + csum[:, 0:1]
            run_ref[h] = run
            run_min = run if run_min is None else jnp.minimum(run_min, run)
        acc_ref[...] += jnp.concatenate(outs, axis=-1)
        return jnp.min(run_min)

    run_min = visit(i, i % SB_RING, causal=True, pad=True)

    def body(carry):
        j, _ = carry
        return j - 1, visit(j, fetch(j), causal=False, pad=False)

    def cond(carry):
        j, run_min = carry
        return (j >= 1) & (run_min < SB_UNDERFLOW)

    j, run_min = lax.while_loop(cond, body, (i - 1, run_min))

    @pl.when((j == 0) & (run_min < SB_UNDERFLOW))
    def _():
        visit(0, fetch(0), causal=False, pad=True)

    o_ref[...] = acc_ref[...].astype(o_ref.dtype)


def _stick_breaking(proj):
    n_rows = proj.shape[0]

    def col(start):
        assert start % D_MODEL == 0
        return pl.BlockSpec((CHUNK, D_MODEL), lambda i, s=start // D_MODEL: (i, s))

    return pl.pallas_call(
        _sb_kernel,
        grid=(n_rows // CHUNK,),
        in_specs=[col(COL_SB_Q), col(COL_SB_K), col(COL_SB_V), pl.BlockSpec(memory_space=pl.ANY)],
        out_specs=pl.BlockSpec((CHUNK, D_MODEL), lambda i: (i, 0)),
        out_shape=jax.ShapeDtypeStruct((n_rows, D_MODEL), BF16),
        scratch_shapes=[
            pltpu.VMEM((SB_RING + 1, CHUNK, D_MODEL), BF16),
            pltpu.VMEM((SB_RING + 1, CHUNK, D_MODEL), BF16),
            pltpu.VMEM((CHUNK, D_MODEL), F32),
            pltpu.VMEM((SB_HEADS, CHUNK, 1), F32),
            pltpu.SemaphoreType.DMA((2,)),
        ],
        compiler_params=pltpu.CompilerParams(
            dimension_semantics=("arbitrary",), vmem_limit_bytes=VMEM_LIMIT),
    )(proj, proj, proj, proj)


def _merge_kernel(ya_ref, yb_ref, yc_ref, yd_ref, ga_ref, gb_ref, gc_ref, gd_ref, h_ref,
                  wb_ref, wo_ref, nw_ref, o_ref):
    merged = None
    branches = ((ya_ref, ga_ref), (yb_ref, gb_ref), (yc_ref, gc_ref), (yd_ref, gd_ref))
    for n, (y_ref, g_ref) in enumerate(branches):
        up = _dot(y_ref[...], wb_ref[n])
        gate = _sigmoid(g_ref[...].astype(F32))
        merged = gate * up if merged is None else merged + gate * up
    mix = _dot(merged.astype(BF16), wo_ref[...])
    o_ref[...] = h_ref[...] + _rmsnorm(mix, nw_ref[...])


def _merge(ya, yb, yc, yd, proj, h, w_branch, w_out, norm_w):
    n_rows = h.shape[0]
    tm = _row_tile(n_rows, 384)
    row_block = pl.BlockSpec((tm, D_MODEL), lambda i: (i, 0))
    return pl.pallas_call(
        _merge_kernel,
        grid=(n_rows // tm,),
        in_specs=[
            row_block, row_block, row_block, row_block,
        ] + [
            pl.BlockSpec((tm, D_MODEL), lambda i, s=COL_GATE // D_MODEL + n: (i, s)) for n in range(N_BRANCH)
        ] + [
            row_block,
            pl.BlockSpec((N_BRANCH, D_MODEL, D_MODEL), lambda i: (0, 0, 0)),
            pl.BlockSpec((D_MODEL, D_MODEL), lambda i: (0, 0)),
            pl.BlockSpec((1, D_MODEL), lambda i: (0, 0)),
        ],
        out_specs=row_block,
        out_shape=jax.ShapeDtypeStruct((n_rows, D_MODEL), F32),
        compiler_params=pltpu.CompilerParams(
            dimension_semantics=("parallel",), vmem_limit_bytes=VMEM_LIMIT),
    )(ya, yb, yc, yd, proj, proj, proj, proj, h, w_branch, w_out, norm_w)


def _ffn_kernel(h_ref, w1_ref, w2_ref, n1_ref, n2_ref, o_ref):
    h = h_ref[...]
    xn = _rmsnorm(h, n1_ref[...]).astype(BF16)
    f = _dot(xn, w1_ref[...])
    act = (_silu(f[:, 0:D_FF]) * f[:, D_FF:2 * D_FF]).astype(BF16)
    o_ref[...] = h + _rmsnorm(_dot(act, w2_ref[...]), n2_ref[...])


def _ffn(h, w1, w2, n1, n2):
    n_rows = h.shape[0]
    tm = _row_tile(n_rows, 384)
    row_block = pl.BlockSpec((tm, D_MODEL), lambda i: (i, 0))
    return pl.pallas_call(
        _ffn_kernel,
        grid=(n_rows // tm,),
        in_specs=[
            row_block,
            pl.BlockSpec((D_MODEL, 2 * D_FF), lambda i: (0, 0)),
            pl.BlockSpec((D_FF, D_MODEL), lambda i: (0, 0)),
            pl.BlockSpec((1, D_MODEL), lambda i: (0, 0)),
            pl.BlockSpec((1, D_MODEL), lambda i: (0, 0)),
        ],
        out_specs=row_block,
        out_shape=jax.ShapeDtypeStruct((n_rows, D_MODEL), F32),
        compiler_params=pltpu.CompilerParams(
            dimension_semantics=("parallel",), vmem_limit_bytes=VMEM_LIMIT),
    )(h, w1, w2, n1, n2)


def _rope_tables(n_rows):
    half = RET_QK_DIM // 2
    inv = ROPE_BASE ** (-jnp.arange(half, dtype=F32) / half)
    ang = jnp.arange(n_rows).astype(F32)[:, None] * inv[None, :]
    return jnp.cos(ang), jnp.sin(ang)


def _head_expand_matrix():
    e = np.zeros((DT_LANES, SSD_INNER), np.float32)
    for h in range(SSD_HEADS):
        e[h, h * SSD_HEAD_DIM:(h + 1) * SSD_HEAD_DIM] = 1.0
    return jnp.asarray(e, BF16)


def _pad_lanes(v):
    return jnp.pad(v.astype(F32), (0, DT_LANES - v.shape[0]))[None, :]


def kernel(x, meta, w_in, conv_a, ssd_conv_w, ssd_conv_b, ssd_dt_bias, ssd_a_log, ssd_d, ssd_norm,
           w_branch, w_out, w_ffn_in, w_ffn_out, norm_mix_pre, norm_mix_post, norm_ffn_pre,
           norm_ffn_post):
    batch, seq, _ = x.shape
    assert batch == 1 and seq % CHUNK == 0
    depth = w_in.shape[0]
    n_rows = seq + CHUNK
    h = jnp.concatenate([jnp.zeros((PAD, D_MODEL), F32), meta.astype(F32), x[0].astype(F32)], axis=0)
    cos, sin = _rope_tables(n_rows)
    expand = _head_expand_matrix()
    dt_end = DT_COL_START + SSD_HEADS
    for l in range(depth):
        w_main = jnp.concatenate([w_in[l, :, :DT_COL_START], w_in[l, :, dt_end:]], axis=1).astype(BF16)
        w_dt = jnp.pad(w_in[l, :, DT_COL_START:dt_end], ((0, 0), (0, DT_LANES - SSD_HEADS))).astype(BF16)
        proj, dt_raw = _inproj(h, norm_mix_pre[l][None, :], w_main, w_dt)
        ya, yb, yc = _mixers(
            proj, dt_raw, cos, sin, conv_a[l], ssd_conv_w[l], ssd_conv_b[l][None, :],
            _pad_lanes(ssd_dt_bias[l]), _pad_lanes(ssd_a_log[l]),
            jnp.repeat(ssd_d[l].astype(F32), SSD_HEAD_DIM)[None, :], ssd_norm[l][None, :], expand)
        yd = _stick_breaking(proj)
        h = _merge(ya, yb, yc, yd, proj, h, w_branch[l].astype(BF16), w_out[l].astype(BF16),
                   norm_mix_post[l][None, :])
        h = _ffn(h, w_ffn_in[l].astype(BF16), w_ffn_out[l].astype(BF16),
                 norm_ffn_pre[l][None, :], norm_ffn_post[l][None, :])
    return h[CHUNK:][None].astype(x.dtype)
```

```python
import functools
import math

import numpy as np
import jax
import jax.numpy as jnp
from jax import lax
from jax.experimental import pallas as pl
from jax.experimental.pallas import tpu as pltpu

F32 = jnp.float32
BF16 = jnp.bfloat16

D_MODEL = 1024
N_META = 16
CHUNK = 128
PAD = CHUNK - N_META
EPS = 1e-6

SSD_HEAD_DIM = 64
SSD_HEADS = 16
SSD_INNER = 1024
SSD_GROUPS = 4
SSD_STATE = 128
SSD_CONV_K = 4
SSD_CONV_DIM = SSD_INNER + 2 * SSD_GROUPS * SSD_STATE
SSD_GROUP_WIDTH = SSD_INNER // SSD_GROUPS
CONV_A_K = 3
RET_HEADS = 4
RET_QK_DIM = 256
RET_V_DIM = 256
ROPE_BASE = 10000.0
SB_HEADS = 8
SB_HEAD_DIM = 128
N_BRANCH = 4
D_FF = 2816

COL_CONV = 0
COL_SSD_Z = 3072
COL_SSD_XBC = 4096
COL_RET_QK = 6144
COL_RET_VG = 8192
COL_SB_Q = 10240
COL_SB_K = 11264
COL_SB_V = 12288
COL_GATE = 13312
PROJ_WIDTH = 17408
DT_COL_START = 6144
DT_LANES = 128

HIST = 8
SB_UNDERFLOW = 104.0
SB_RING = 4
VMEM_LIMIT = 56 * 1024 * 1024


def _rmsnorm(x, w):
    return x * lax.rsqrt(jnp.mean(x * x, axis=-1, keepdims=True) + EPS) * w


def _sigmoid(x):
    return 1.0 / (1.0 + jnp.exp(-x))


def _silu(x):
    return x * _sigmoid(x)


def _softplus(x):
    return jnp.maximum(x, 0.0) + jnp.log1p(jnp.exp(-jnp.abs(x)))


def _split3(x):
    hi = x.astype(BF16)
    r = x - hi.astype(F32)
    mid = r.astype(BF16)
    lo = (r - mid.astype(F32)).astype(BF16)
    return hi, mid, lo


def _dot(a, b):
    return jnp.dot(a, b, preferred_element_type=F32)


def _dot_nt(a, b):
    return lax.dot_general(a, b, (((1,), (1,)), ((), ())), preferred_element_type=F32)


def _dot_tn(a, b):
    return lax.dot_general(a, b, (((0,), (0,)), ((), ())), preferred_element_type=F32)


def _row_tile(n_rows, target):
    best = 16
    for t in range(16, min(n_rows, target) + 1, 16):
        if n_rows % t == 0:
            best = t
    return best


def _inproj_kernel(x_ref, nw_ref, w_ref, wdt_ref, o_ref, dt_ref, xn_ref):
    @pl.when(pl.program_id(1) == 0)
    def _():
        xn = _rmsnorm(x_ref[...], nw_ref[...]).astype(BF16)
        xn_ref[...] = xn
        dt_ref[...] = _dot(xn, wdt_ref[...])

    o_ref[...] = _dot(xn_ref[...], w_ref[...]).astype(o_ref.dtype)


def _inproj(h, norm_w, w_main, w_dt, layer):
    n_rows = h.shape[0]
    tm = _row_tile(n_rows, 1376)
    tn = 1024
    return pl.pallas_call(
        _inproj_kernel,
        grid=(n_rows // tm, PROJ_WIDTH // tn),
        in_specs=[
            pl.BlockSpec((tm, D_MODEL), lambda i, j: (i, 0)),
            pl.BlockSpec((1, D_MODEL), lambda i, j: (0, 0)),
            pl.BlockSpec((None, D_MODEL, tn), lambda i, j: (layer, 0, j)),
            pl.BlockSpec((None, D_MODEL, DT_LANES), lambda i, j: (layer, 0, 0)),
        ],
        out_specs=[
            pl.BlockSpec((tm, tn), lambda i, j: (i, j)),
            pl.BlockSpec((tm, DT_LANES), lambda i, j: (i, 0)),
        ],
        out_shape=[
            jax.ShapeDtypeStruct((n_rows, PROJ_WIDTH), BF16),
            jax.ShapeDtypeStruct((n_rows, DT_LANES), F32),
        ],
        scratch_shapes=[pltpu.VMEM((tm, D_MODEL), BF16)],
        compiler_params=pltpu.CompilerParams(
            dimension_semantics=("parallel", "arbitrary"), vmem_limit_bytes=VMEM_LIMIT),
    )(h, norm_w, w_main, w_dt)


def _mixer_kernel(conv_ref, z_ref, xbc_ref, qk_ref, vg_ref, dt_ref, ccos_ref, csin_ref,
                  bcos_ref, bsin_ref,
                  conva_ref, sconvw_ref, sconvb_ref, dtb_ref, alog_ref, dskip_ref, snorm_ref,
                  expand_ref,
                  ya_ref, yb_ref, yc_ref,
                  uhist_ref, xhist_ref, sstate_ref, rstate_ref, dmask_ref, kdec_ref, qdec_ref):
    c = pl.program_id(0)
    li = lax.broadcasted_iota(jnp.int32, (CHUNK, CHUNK), 0)
    si = lax.broadcasted_iota(jnp.int32, (CHUNK, CHUNK), 1)
    causal = li >= si

    @pl.when(c == 0)
    def _():
        uhist_ref[0:HIST, :] = jnp.zeros((HIST, D_MODEL), F32)
        xhist_ref[0:HIST, :] = jnp.zeros((HIST, SSD_CONV_DIM), F32)
        sstate_ref[...] = jnp.zeros_like(sstate_ref)
        rstate_ref[...] = jnp.zeros_like(rstate_ref)
        rel = (li - si).astype(F32)
        row_w = lax.broadcasted_iota(jnp.int32, (CHUNK, RET_QK_DIM), 0).astype(F32)
        for h in range(RET_HEADS):
            log_gamma = math.log(1.0 - 2.0 ** (-5.0 - h))
            dmask_ref[h] = jnp.where(causal, jnp.exp(log_gamma * jnp.maximum(rel, 0.0)), 0.0)
            kdec_ref[h] = jnp.exp(log_gamma * (CHUNK - 1.0 - row_w))
            qdec_ref[h] = jnp.exp(log_gamma * (row_w + 1.0))

    def valid(shape):
        return (lax.broadcasted_iota(jnp.int32, shape, 0) + c * CHUNK) >= PAD

    b_gate = conv_ref[:, 0:D_MODEL].astype(F32)
    c_gate = conv_ref[:, D_MODEL:2 * D_MODEL].astype(F32)
    xa = conv_ref[:, 2 * D_MODEL:3 * D_MODEL].astype(F32)
    u = jnp.where(valid((CHUNK, D_MODEL)), c_gate * xa, 0.0)
    uhist_ref[HIST:HIST + CHUNK, :] = u
    conv = u * conva_ref[CONV_A_K - 1:CONV_A_K, :]
    for i in range(CONV_A_K - 1):
        off = HIST - (CONV_A_K - 1) + i
        conv = conv + uhist_ref[off:off + CHUNK, :] * conva_ref[i:i + 1, :]
    ya_ref[...] = (b_gate * conv).astype(ya_ref.dtype)
    uhist_ref[0:HIST, :] = uhist_ref[CHUNK:CHUNK + HIST, :]

    xin = jnp.where(valid((CHUNK, SSD_CONV_DIM)), xbc_ref[...].astype(F32), 0.0)
    xhist_ref[HIST:HIST + CHUNK, :] = xin
    xc = xin * sconvw_ref[SSD_CONV_K - 1:SSD_CONV_K, :] + sconvb_ref[...]
    for i in range(SSD_CONV_K - 1):
        off = HIST - (SSD_CONV_K - 1) + i
        xc = xc + xhist_ref[off:off + CHUNK, :] * sconvw_ref[i:i + 1, :]
    xhist_ref[0:HIST, :] = xhist_ref[CHUNK:CHUNK + HIST, :]
    xc = _silu(xc)
    xs = jnp.where(valid((CHUNK, SSD_INNER)), xc[:, 0:SSD_INNER], 0.0)
    bm = xc[:, SSD_INNER:SSD_INNER + SSD_GROUPS * SSD_STATE].astype(BF16)
    cm = xc[:, SSD_INNER + SSD_GROUPS * SSD_STATE:SSD_CONV_DIM].astype(BF16)

    dtv = _softplus(dt_ref[...] + dtb_ref[...])
    a = -jnp.exp(alog_ref[...]) * dtv
    tri = causal.astype(BF16)
    acs = sum(_dot(tri, part) for part in _split3(a))
    acs_t = acs.T
    acs_last = acs[CHUNK - 1:CHUNK, :]
    expand = expand_ref[...]

    def expand_heads(v):
        return sum(_dot(part, expand) for part in _split3(v))

    dt_x = expand_heads(dtv)
    eacs_x = jnp.exp(expand_heads(acs))
    dstate_x = jnp.exp(expand_heads(acs_last - acs))
    cdecay_x = eacs_x[CHUNK - 1:CHUNK, :]
    xdt = xs * dt_x
    xdt_b = xdt.astype(BF16)
    xw_b = (xdt * dstate_x).astype(BF16)

    hpg = SSD_HEADS // SSD_GROUPS
    groups = [slice(g * SSD_GROUP_WIDTH, (g + 1) * SSD_GROUP_WIDTH) for g in range(SSD_GROUPS)]
    ccs = [cm[:, g * SSD_STATE:(g + 1) * SSD_STATE] for g in range(SSD_GROUPS)]
    bcs = [bm[:, g * SSD_STATE:(g + 1) * SSD_STATE] for g in range(SSD_GROUPS)]
    cbs = [_dot_nt(cc, bc) for cc, bc in zip(ccs, bcs)]
    prevs = [sstate_ref[g] for g in range(SSD_GROUPS)]
    y_offs = [_dot(cc, prev.astype(BF16)) for cc, prev in zip(ccs, prevs)]
    new_states = [_dot_tn(bc, xw_b[:, gs]) for bc, gs in zip(bcs, groups)]
    for g in range(SSD_GROUPS):
        sstate_ref[g] = prevs[g] * cdecay_x[:, groups[g]] + new_states[g]
    yd = []
    for h in range(SSD_HEADS):
        diff = acs[:, h:h + 1] - acs_t[h:h + 1, :]
        seg = jnp.exp(jnp.where(causal, diff, -jnp.inf))
        hs = slice(h * SSD_HEAD_DIM, (h + 1) * SSD_HEAD_DIM)
        yd.append(_dot((cbs[h // hpg] * seg).astype(BF16), xdt_b[:, hs]))
    y = jnp.concatenate(yd, axis=-1) + jnp.concatenate(y_offs, axis=-1) * eacs_x + xs * dskip_ref[...]
    y = y * _silu(z_ref[...].astype(F32))
    yn = []
    for g in range(SSD_GROUPS):
        yg = y[:, g * SSD_GROUP_WIDTH:(g + 1) * SSD_GROUP_WIDTH]
        yn.append(yg * lax.rsqrt(jnp.mean(yg * yg, axis=-1, keepdims=True) + EPS))
    yb_ref[...] = (jnp.concatenate(yn, axis=-1) * snorm_ref[...]).astype(yb_ref.dtype)

    cc_, sc_ = ccos_ref[0], csin_ref[0]
    bcos, bsin = bcos_ref[...], bsin_ref[...]
    cos = cc_ * bcos - sc_ * bsin
    sin = sc_ * bcos + cc_ * bsin
    half = RET_QK_DIM // 2
    ko = RET_HEADS * RET_QK_DIM
    go = RET_HEADS * RET_V_DIM
    v_valid = valid((CHUNK, RET_V_DIM))

    def rotated(off):
        x1 = qk_ref[:, off:off + half].astype(F32)
        x2 = qk_ref[:, off + half:off + RET_QK_DIM].astype(F32)
        return jnp.concatenate([x1 * cos - x2 * sin, x1 * sin + x2 * cos], axis=-1)

    qrs, krs, vrs, prevs = [], [], [], []
    for h in range(RET_HEADS):
        qrs.append(rotated(h * RET_QK_DIM).astype(BF16))
        krs.append(rotated(ko + h * RET_QK_DIM) * (RET_QK_DIM ** -0.5))
        vrs.append(jnp.where(v_valid, vg_ref[:, h * RET_V_DIM:(h + 1) * RET_V_DIM].astype(F32), 0.0).astype(BF16))
        prevs.append(rstate_ref[h])
    scores = [_dot_nt(qr, kr.astype(BF16)) for qr, kr in zip(qrs, krs)]
    y_crs = [_dot(qr, prev.astype(BF16)) for qr, prev in zip(qrs, prevs)]
    kvs = [_dot_tn((krs[h] * kdec_ref[h]).astype(BF16), vrs[h]) for h in range(RET_HEADS)]
    for h in range(RET_HEADS):
        log_gamma = math.log(1.0 - 2.0 ** (-5.0 - h))
        rstate_ref[h] = prevs[h] * math.exp(log_gamma * CHUNK) + kvs[h]
    y_ins = [_dot((scores[h] * dmask_ref[h]).astype(BF16), vrs[h]) for h in range(RET_HEADS)]
    yc = []
    for h in range(RET_HEADS):
        gate = vg_ref[:, go + h * RET_V_DIM:go + (h + 1) * RET_V_DIM].astype(F32)
        yh = y_ins[h] + y_crs[h] * qdec_ref[h]
        mu = jnp.mean(yh, axis=-1, keepdims=True)
        yh = yh - mu
        var = jnp.mean(yh * yh, axis=-1, keepdims=True)
        yc.append(yh * lax.rsqrt(var + EPS) * _silu(gate))
    yc_ref[...] = jnp.concatenate(yc, axis=-1).astype(yc_ref.dtype)


def _mixers(proj, dt_raw, rope, conv_a, sconv_w, sconv_b, dt_bias, a_log, d_skip, snorm, expand):
    n_rows = proj.shape[0]
    nc = n_rows // CHUNK
    chunk_cos, chunk_sin, base_cos, base_sin = rope

    def col(width, start):
        assert start % width == 0
        return pl.BlockSpec((CHUNK, width), lambda c, s=start // width: (c, s))

    def full(arr):
        return pl.BlockSpec(arr.shape, lambda c: (0,) * arr.ndim)

    row_block = pl.BlockSpec((CHUNK, D_MODEL), lambda c: (c, 0))
    chunk_row = pl.BlockSpec((1, 1, RET_QK_DIM // 2), lambda c: (c, 0, 0))
    weights = (base_cos, base_sin, conv_a, sconv_w, sconv_b, dt_bias, a_log, d_skip, snorm, expand)
    return pl.pallas_call(
        _mixer_kernel,
        grid=(nc,),
        in_specs=[
            col(3 * D_MODEL, COL_CONV), col(D_MODEL, COL_SSD_Z), col(SSD_CONV_DIM, COL_SSD_XBC),
            col(2048, COL_RET_QK), col(2048, COL_RET_VG),
            pl.BlockSpec((CHUNK, DT_LANES), lambda c: (c, 0)), chunk_row, chunk_row,
        ] + [full(w) for w in weights],
        out_specs=[row_block, row_block, row_block],
        out_shape=[jax.ShapeDtypeStruct((n_rows, D_MODEL), BF16)] * 3,
        scratch_shapes=[
            pltpu.VMEM((HIST + CHUNK, D_MODEL), F32),
            pltpu.VMEM((HIST + CHUNK, SSD_CONV_DIM), F32),
            pltpu.VMEM((SSD_GROUPS, SSD_STATE, SSD_GROUP_WIDTH), F32),
            pltpu.VMEM((RET_HEADS, RET_QK_DIM, RET_V_DIM), F32),
            pltpu.VMEM((RET_HEADS, CHUNK, CHUNK), F32),
            pltpu.VMEM((RET_HEADS, CHUNK, RET_QK_DIM), F32),
            pltpu.VMEM((RET_HEADS, CHUNK, RET_QK_DIM), F32),
        ],
        compiler_params=pltpu.CompilerParams(
            dimension_semantics=("arbitrary",), vmem_limit_bytes=VMEM_LIMIT),
    )(proj, proj, proj, proj, proj, dt_raw, chunk_cos, chunk_sin, *weights)


def _sb_kernel(q_ref, k_ref, v_ref, proj_hbm, o_ref, kbuf, vbuf, acc_ref, run_ref, sem):
    i = pl.program_id(0)
    kbuf[i % SB_RING] = k_ref[...]
    vbuf[i % SB_RING] = v_ref[...]
    acc_ref[...] = jnp.zeros_like(acc_ref)
    run_ref[...] = jnp.zeros_like(run_ref)
    rows = lax.broadcasted_iota(jnp.int32, (CHUNK, CHUNK), 0)
    cols = lax.broadcasted_iota(jnp.int32, (CHUNK, CHUNK), 1)
    suffix = (rows >= cols).astype(BF16)
    scale = SB_HEAD_DIM ** -0.5

    def fetch(j):
        far = (i - j) >= SB_RING

        @pl.when(far)
        def _():
            rows_j = pl.ds(pl.multiple_of(j * CHUNK, CHUNK), CHUNK)
            copies = [
                pltpu.make_async_copy(proj_hbm.at[rows_j, pl.ds(COL_SB_K, D_MODEL)], kbuf.at[SB_RING], sem.at[0]),
                pltpu.make_async_copy(proj_hbm.at[rows_j, pl.ds(COL_SB_V, D_MODEL)], vbuf.at[SB_RING], sem.at[1]),
            ]
            for cp in copies:
                cp.start()
            for cp in copies:
                cp.wait()

        return jnp.where(far, SB_RING, j % SB_RING)

    def visit(j, slot, causal, pad):
        mask = None
        if causal:
            mask = cols < rows
        if pad:
            in_seq = (cols + j * CHUNK) >= PAD
            mask = in_seq if mask is None else mask & in_seq
        heads = [slice(h * SB_HEAD_DIM, (h + 1) * SB_HEAD_DIM) for h in range(SB_HEADS)]
        zs = [_dot_nt(q_ref[:, hs], kbuf[slot, :, hs]) * scale for hs in heads]
        csums = []
        for z in zs:
            sp = jnp.maximum(z, 0.0) + jnp.log(1.0 + jnp.exp(-jnp.abs(z)))
            if mask is not None:
                sp = jnp.where(mask, sp, 0.0)
            sp_hi = sp.astype(BF16)
            sp_lo = (sp - sp_hi.astype(F32)).astype(BF16)
            csums.append(_dot(sp_hi, suffix) + _dot(sp_lo, suffix))
        run_min = None
        outs = []
        for h, (z, csum) in enumerate(zip(zs, csums)):
            run = run_ref[h]
            w = jnp.exp(z - csum - run)
            if mask is not None:
                w = jnp.where(mask, w, 0.0)
            outs.append(_dot(w.astype(BF16), vbuf[slot, :, heads[h]]))
            run = run + csum[:, 0:1]
            run_ref[h] = run
            run_min = run if run_min is None else jnp.minimum(run_min, run)
        acc_ref[...] += jnp.concatenate(outs, axis=-1)
        return jnp.min(run_min)

    run_min = visit(i, i % SB_RING, causal=True, pad=True)

    def body(carry):
        j, _ = carry
        return j - 1, visit(j, fetch(j), causal=False, pad=False)

    def cond(carry):
        j, run_min = carry
        return (j >= 1) & (run_min < SB_UNDERFLOW)

    j, run_min = lax.while_loop(cond, body, (i - 1, run_min))

    @pl.when((j == 0) & (run_min < SB_UNDERFLOW))
    def _():
        visit(0, fetch(0), causal=False, pad=True)

    o_ref[...] = acc_ref[...].astype(o_ref.dtype)


def _stick_breaking(proj):
    n_rows = proj.shape[0]

    def col(start):
        assert start % D_MODEL == 0
        return pl.BlockSpec((CHUNK, D_MODEL), lambda i, s=start // D_MODEL: (i, s))

    return pl.pallas_call(
        _sb_kernel,
        grid=(n_rows // CHUNK,),
        in_specs=[col(COL_SB_Q), col(COL_SB_K), col(COL_SB_V), pl.BlockSpec(memory_space=pl.ANY)],
        out_specs=pl.BlockSpec((CHUNK, D_MODEL), lambda i: (i, 0)),
        out_shape=jax.ShapeDtypeStruct((n_rows, D_MODEL), BF16),
        scratch_shapes=[
            pltpu.VMEM((SB_RING + 1, CHUNK, D_MODEL), BF16),
            pltpu.VMEM((SB_RING + 1, CHUNK, D_MODEL), BF16),
            pltpu.VMEM((CHUNK, D_MODEL), F32),
            pltpu.VMEM((SB_HEADS, CHUNK, 1), F32),
            pltpu.SemaphoreType.DMA((2,)),
        ],
        compiler_params=pltpu.CompilerParams(
            dimension_semantics=("arbitrary",), vmem_limit_bytes=VMEM_LIMIT),
    )(proj, proj, proj, proj)


def _merge_kernel(ya_ref, yb_ref, yc_ref, yd_ref, ga_ref, gb_ref, gc_ref, gd_ref, h_ref,
                  wb_ref, wo_ref, nw_ref, o_ref):
    merged = None
    branches = ((ya_ref, ga_ref), (yb_ref, gb_ref), (yc_ref, gc_ref), (yd_ref, gd_ref))
    for n, (y_ref, g_ref) in enumerate(branches):
        up = _dot(y_ref[...], wb_ref[n])
        gate = _sigmoid(g_ref[...].astype(F32))
        merged = gate * up if merged is None else merged + gate * up
    mix = _dot(merged.astype(BF16), wo_ref[...])
    o_ref[...] = h_ref[...] + _rmsnorm(mix, nw_ref[...])


def _merge(ya, yb, yc, yd, proj, h, w_branch, w_out, norm_w, layer):
    n_rows = h.shape[0]
    tm = _row_tile(n_rows, 384)
    row_block = pl.BlockSpec((tm, D_MODEL), lambda i: (i, 0))
    return pl.pallas_call(
        _merge_kernel,
        grid=(n_rows // tm,),
        in_specs=[
            row_block, row_block, row_block, row_block,
        ] + [
            pl.BlockSpec((tm, D_MODEL), lambda i, s=COL_GATE // D_MODEL + n: (i, s)) for n in range(N_BRANCH)
        ] + [
            row_block,
            pl.BlockSpec((None, N_BRANCH, D_MODEL, D_MODEL), lambda i: (layer, 0, 0, 0)),
            pl.BlockSpec((None, D_MODEL, D_MODEL), lambda i: (layer, 0, 0)),
            pl.BlockSpec((1, D_MODEL), lambda i: (0, 0)),
        ],
        out_specs=row_block,
        out_shape=jax.ShapeDtypeStruct((n_rows, D_MODEL), F32),
        compiler_params=pltpu.CompilerParams(
            dimension_semantics=("parallel",), vmem_limit_bytes=VMEM_LIMIT),
    )(ya, yb, yc, yd, proj, proj, proj, proj, h, w_branch, w_out, norm_w)


def _ffn_kernel(h_ref, w1_ref, w2_ref, n1_ref, n2_ref, o_ref):
    h = h_ref[...]
    xn = _rmsnorm(h, n1_ref[...]).astype(BF16)
    f = _dot(xn, w1_ref[...])
    act = (_silu(f[:, 0:D_FF]) * f[:, D_FF:2 * D_FF]).astype(BF16)
    o_ref[...] = h + _rmsnorm(_dot(act, w2_ref[...]), n2_ref[...])


def _ffn(h, w1, w2, n1, n2, layer):
    n_rows = h.shape[0]
    tm = _row_tile(n_rows, 384)
    row_block = pl.BlockSpec((tm, D_MODEL), lambda i: (i, 0))
    return pl.pallas_call(
        _ffn_kernel,
        grid=(n_rows // tm,),
        in_specs=[
            row_block,
            pl.BlockSpec((None, D_MODEL, 2 * D_FF), lambda i: (layer, 0, 0)),
            pl.BlockSpec((None, D_FF, D_MODEL), lambda i: (layer, 0, 0)),
            pl.BlockSpec((1, D_MODEL), lambda i: (0, 0)),
            pl.BlockSpec((1, D_MODEL), lambda i: (0, 0)),
        ],
        out_specs=row_block,
        out_shape=jax.ShapeDtypeStruct((n_rows, D_MODEL), F32),
        compiler_params=pltpu.CompilerParams(
            dimension_semantics=("parallel",), vmem_limit_bytes=VMEM_LIMIT),
    )(h, w1, w2, n1, n2)


def _rope_tables(n_chunks):
    half = RET_QK_DIM // 2
    inv = ROPE_BASE ** (-np.arange(half, dtype=np.float64) / half)
    chunk_ang = (np.arange(n_chunks, dtype=np.float64) * CHUNK)[:, None, None] * inv[None, None, :]
    base_ang = np.arange(CHUNK, dtype=np.float64)[:, None] * inv[None, :]
    return tuple(jnp.asarray(t, F32) for t in
                 (np.cos(chunk_ang), np.sin(chunk_ang), np.cos(base_ang), np.sin(base_ang)))


def _head_expand_matrix():
    e = np.zeros((DT_LANES, SSD_INNER), np.float32)
    for h in range(SSD_HEADS):
        e[h, h * SSD_HEAD_DIM:(h + 1) * SSD_HEAD_DIM] = 1.0
    return jnp.asarray(e, BF16)


def _pad_lanes(v):
    return jnp.pad(v.astype(F32), (0, DT_LANES - v.shape[0]))[None, :]


def kernel(x, meta, w_in, conv_a, ssd_conv_w, ssd_conv_b, ssd_dt_bias, ssd_a_log, ssd_d, ssd_norm,
           w_branch, w_out, w_ffn_in, w_ffn_out, norm_mix_pre, norm_mix_post, norm_ffn_pre,
           norm_ffn_post):
    batch, seq, _ = x.shape
    assert batch == 1 and seq % CHUNK == 0
    depth = w_in.shape[0]
    n_rows = seq + CHUNK
    h = jnp.concatenate([jnp.zeros((PAD, D_MODEL), F32), meta.astype(F32), x[0].astype(F32)], axis=0)
    rope = _rope_tables(n_rows // CHUNK)
    expand = _head_expand_matrix()
    dt_end = DT_COL_START + SSD_HEADS
    w_main = jnp.concatenate([w_in[:, :, :DT_COL_START], w_in[:, :, dt_end:]], axis=2).astype(BF16)
    w_dt = jnp.pad(w_in[:, :, DT_COL_START:dt_end], ((0, 0), (0, 0), (0, DT_LANES - SSD_HEADS))).astype(BF16)
    w_branch_b, w_out_b = w_branch.astype(BF16), w_out.astype(BF16)
    w_ffn_in_b, w_ffn_out_b = w_ffn_in.astype(BF16), w_ffn_out.astype(BF16)
    for l in range(depth):
        proj, dt_raw = _inproj(h, norm_mix_pre[l][None, :], w_main, w_dt, l)
        ya, yb, yc = _mixers(
            proj, dt_raw, rope, conv_a[l], ssd_conv_w[l], ssd_conv_b[l][None, :],
            _pad_lanes(ssd_dt_bias[l]), _pad_lanes(ssd_a_log[l]),
            jnp.repeat(ssd_d[l].astype(F32), SSD_HEAD_DIM)[None, :], ssd_norm[l][None, :], expand)
        yd = _stick_breaking(proj)
        h = _merge(ya, yb, yc, yd, proj, h, w_branch_b, w_out_b, norm_mix_post[l][None, :], l)
        h = _ffn(h, w_ffn_in_b, w_ffn_out_b, norm_ffn_pre[l][None, :], norm_ffn_post[l][None, :], l)
    return h[CHUNK:][None].astype(x.dtype)
```

```python
import functools
import math

import numpy as np
import jax
import jax.numpy as jnp
from jax import lax
from jax.experimental import pallas as pl
from jax.experimental.pallas import tpu as pltpu

F32 = jnp.float32
BF16 = jnp.bfloat16

D_MODEL = 1024
N_META = 16
CHUNK = 128
PAD = CHUNK - N_META
EPS = 1e-6

SSD_HEAD_DIM = 64
SSD_HEADS = 16
SSD_INNER = 1024
SSD_GROUPS = 4
SSD_STATE = 128
SSD_CONV_K = 4
SSD_CONV_DIM = SSD_INNER + 2 * SSD_GROUPS * SSD_STATE
SSD_GROUP_WIDTH = SSD_INNER // SSD_GROUPS
CONV_A_K = 3
RET_HEADS = 4
RET_QK_DIM = 256
RET_V_DIM = 256
ROPE_BASE = 10000.0
SB_HEADS = 8
SB_HEAD_DIM = 128
N_BRANCH = 4
D_FF = 2816

COL_RET_QK = 0
COL_RET_VG = 2048
COL_SSD_Z = 4096
COL_SB_Q = 5120
COL_SB_K = 6144
COL_SB_V = 7168
COL_GATE = 8192
PROJ_WIDTH = 12288
DT_COL_START = 6144
DT_LANES = 128
INPROJ_A_TILES = 6
CONV_STRIP = 256

HIST = 8
SB_UNDERFLOW = 104.0
SB_RING = 4
VMEM_LIMIT = 56 * 1024 * 1024


def _rmsnorm(x, w):
    return x * lax.rsqrt(jnp.mean(x * x, axis=-1, keepdims=True) + EPS) * w


def _sigmoid(x):
    return 1.0 / (1.0 + jnp.exp(-x))


def _silu(x):
    return x * _sigmoid(x)


def _softplus(x):
    return jnp.maximum(x, 0.0) + jnp.log1p(jnp.exp(-jnp.abs(x)))


def _split3(x):
    hi = x.astype(BF16)
    r = x - hi.astype(F32)
    mid = r.astype(BF16)
    lo = (r - mid.astype(F32)).astype(BF16)
    return hi, mid, lo


def _dot(a, b):
    return jnp.dot(a, b, preferred_element_type=F32)


def _dot_nt(a, b):
    return lax.dot_general(a, b, (((1,), (1,)), ((), ())), preferred_element_type=F32)


def _dot_tn(a, b):
    return lax.dot_general(a, b, (((0,), (0,)), ((), ())), preferred_element_type=F32)


def _row_tile(n_rows, target):
    best = 16
    for t in range(16, min(n_rows, target) + 1, 16):
        if n_rows % t == 0:
            best = t
    return best


def _inproj_kernel(x_ref, nw_ref, wa_ref, wb_ref, wdt_ref, conva_ref, sconvw_ref, sconvb_ref,
                   o_ref, dt_ref, ya_ref, xbc_ref,
                   xn_ref, bgate_ref, cgate_ref, stage_ref, hist_ref):
    i = pl.program_id(0)
    j = pl.program_id(1)
    tm = x_ref.shape[0]
    strips = [slice(n * CONV_STRIP, (n + 1) * CONV_STRIP) for n in range(D_MODEL // CONV_STRIP)]
    valid = (lax.broadcasted_iota(jnp.int32, (tm, CONV_STRIP), 0) + i * tm) >= PAD

    def causal_conv(p, slot, cs, taps):
        k_taps = taps.shape[0]
        stage_ref[0:HIST, cs] = hist_ref[slot, :, cs]
        stage_ref[HIST:HIST + tm, cs] = p
        out = p * taps[k_taps - 1:k_taps, :]
        for t in range(k_taps - 1):
            off = HIST - (k_taps - 1) + t
            out = out + stage_ref[off:off + tm, cs] * taps[t:t + 1, :]
        hist_ref[slot, :, cs] = stage_ref[tm:tm + HIST, cs]
        return out

    def pipelined_strips():
        nxt = _dot(xn_ref[...], wa_ref[:, strips[0]])
        for n, cs in enumerate(strips):
            cur = nxt
            if n + 1 < len(strips):
                nxt = _dot(xn_ref[...], wa_ref[:, strips[n + 1]])
            yield cs, cur

    @pl.when((i == 0) & (j == 0))
    def _():
        hist_ref[...] = jnp.zeros_like(hist_ref)

    @pl.when(j == 0)
    def _():
        xn = _rmsnorm(x_ref[...], nw_ref[...]).astype(BF16)
        xn_ref[...] = xn
        dt_ref[...] = _dot(xn, wdt_ref[...])
        bgate_ref[...] = _dot(xn, wa_ref[...]).astype(BF16)

    @pl.when(j == 1)
    def _():
        cgate_ref[...] = _dot(xn_ref[...], wa_ref[...]).astype(BF16)

    @pl.when(j == 2)
    def _():
        for cs, xa in pipelined_strips():
            u = jnp.where(valid, cgate_ref[:, cs].astype(F32) * xa, 0.0)
            conv = causal_conv(u, 0, cs, conva_ref[:, cs])
            ya_ref[:, cs] = (bgate_ref[:, cs].astype(F32) * conv).astype(ya_ref.dtype)

    @pl.when(j == 3)
    def _():
        o_ref[...] = _dot(xn_ref[...], wa_ref[...]).astype(o_ref.dtype)

    for tile in range(2):
        @pl.when(j == 4 + tile)
        def _(tile=tile):
            for cs, p in pipelined_strips():
                ws = slice(tile * D_MODEL + cs.start, tile * D_MODEL + cs.stop)
                p = jnp.where(valid, p, 0.0)
                xc = _silu(causal_conv(p, 1 + tile, cs, sconvw_ref[:, ws]) + sconvb_ref[:, ws])
                if tile == 0:
                    xc = jnp.where(valid, xc, 0.0)
                xbc_ref[:, cs] = xc.astype(xbc_ref.dtype)

    @pl.when(j >= INPROJ_A_TILES)
    def _():
        o_ref[...] = _dot(xn_ref[...], wb_ref[...]).astype(o_ref.dtype)


def _inproj(h, norm_w, w_a, w_b, w_dt, conv_a, sconv_w, sconv_b, layer):
    n_rows = h.shape[0]
    tm = _row_tile(n_rows, 688)
    tn = D_MODEL
    n_b = w_b.shape[2] // tn
    na = INPROJ_A_TILES

    def out_tile(j):
        return jnp.where(j <= 5, COL_SSD_Z // tn, jnp.where(j <= 9, j - 6, j - 5))

    def full(arr):
        return pl.BlockSpec(arr.shape, lambda i, j: (0,) * arr.ndim)

    return pl.pallas_call(
        _inproj_kernel,
        grid=(n_rows // tm, na + n_b),
        in_specs=[
            pl.BlockSpec((tm, D_MODEL), lambda i, j: (i, 0)),
            pl.BlockSpec((1, D_MODEL), lambda i, j: (0, 0)),
            pl.BlockSpec((None, D_MODEL, tn), lambda i, j: (layer, 0, jnp.minimum(j, na - 1))),
            pl.BlockSpec((None, D_MODEL, tn), lambda i, j: (layer, 0, jnp.maximum(j - na, 0))),
            pl.BlockSpec((None, D_MODEL, DT_LANES), lambda i, j: (layer, 0, 0)),
            full(conv_a), full(sconv_w), full(sconv_b),
        ],
        out_specs=[
            pl.BlockSpec((tm, tn), lambda i, j: (i, out_tile(j))),
            pl.BlockSpec((tm, DT_LANES), lambda i, j: (i, 0)),
            pl.BlockSpec((tm, D_MODEL), lambda i, j: (i, 0)),
            pl.BlockSpec((tm, D_MODEL), lambda i, j: (i, jnp.where(j <= 4, 0, 1))),
        ],
        out_shape=[
            jax.ShapeDtypeStruct((n_rows, PROJ_WIDTH), BF16),
            jax.ShapeDtypeStruct((n_rows, DT_LANES), F32),
            jax.ShapeDtypeStruct((n_rows, D_MODEL), BF16),
            jax.ShapeDtypeStruct((n_rows, SSD_CONV_DIM), BF16),
        ],
        scratch_shapes=[
            pltpu.VMEM((tm, D_MODEL), BF16),
            pltpu.VMEM((tm, D_MODEL), BF16),
            pltpu.VMEM((tm, D_MODEL), BF16),
            pltpu.VMEM((HIST + tm, D_MODEL), F32),
            pltpu.VMEM((3, HIST, D_MODEL), F32),
        ],
        compiler_params=pltpu.CompilerParams(
            dimension_semantics=("arbitrary", "arbitrary"), vmem_limit_bytes=VMEM_LIMIT),
    )(h, norm_w, w_a, w_b, w_dt, conv_a, sconv_w, sconv_b)


def _mixer_kernel(z_ref, xbc_ref, qk_ref, vg_ref, dt_ref, ccos_ref, csin_ref,
                  bcos_ref, bsin_ref, dtb_ref, alog_ref, dskip_ref, snorm_ref, expand_ref,
                  yb_ref, yc_ref,
                  sstate_ref, rstate_ref, dmask_ref, kdec_ref, qdec_ref):
    c = pl.program_id(0)
    li = lax.broadcasted_iota(jnp.int32, (CHUNK, CHUNK), 0)
    si = lax.broadcasted_iota(jnp.int32, (CHUNK, CHUNK), 1)
    causal = li >= si

    @pl.when(c == 0)
    def _():
        sstate_ref[...] = jnp.zeros_like(sstate_ref)
        rstate_ref[...] = jnp.zeros_like(rstate_ref)
        rel = (li - si).astype(F32)
        row_w = lax.broadcasted_iota(jnp.int32, (CHUNK, RET_QK_DIM), 0).astype(F32)
        for h in range(RET_HEADS):
            log_gamma = math.log(1.0 - 2.0 ** (-5.0 - h))
            dmask_ref[h] = jnp.where(causal, jnp.exp(log_gamma * jnp.maximum(rel, 0.0)), 0.0)
            kdec_ref[h] = jnp.exp(log_gamma * (CHUNK - 1.0 - row_w))
            qdec_ref[h] = jnp.exp(log_gamma * (row_w + 1.0))

    def valid(shape):
        return (lax.broadcasted_iota(jnp.int32, shape, 0) + c * CHUNK) >= PAD

    xs = xbc_ref[:, 0:SSD_INNER].astype(F32)
    bm = xbc_ref[:, SSD_INNER:SSD_INNER + SSD_GROUPS * SSD_STATE]
    cm = xbc_ref[:, SSD_INNER + SSD_GROUPS * SSD_STATE:SSD_CONV_DIM]

    dtv = _softplus(dt_ref[...] + dtb_ref[...])
    a = -jnp.exp(alog_ref[...]) * dtv
    tri = causal.astype(BF16)
    acs = sum(_dot(tri, part) for part in _split3(a))
    acs_t = acs.T
    acs_last = acs[CHUNK - 1:CHUNK, :]
    expand = expand_ref[...]

    def expand_heads(v):
        return sum(_dot(part, expand) for part in _split3(v))

    dt_x = expand_heads(dtv)
    eacs_x = jnp.exp(expand_heads(acs))
    dstate_x = jnp.exp(expand_heads(acs_last - acs))
    cdecay_x = eacs_x[CHUNK - 1:CHUNK, :]
    xdt = xs * dt_x
    xdt_b = xdt.astype(BF16)
    xw_b = (xdt * dstate_x).astype(BF16)

    hpg = SSD_HEADS // SSD_GROUPS
    groups = [slice(g * SSD_GROUP_WIDTH, (g + 1) * SSD_GROUP_WIDTH) for g in range(SSD_GROUPS)]
    ccs = [cm[:, g * SSD_STATE:(g + 1) * SSD_STATE] for g in range(SSD_GROUPS)]
    bcs = [bm[:, g * SSD_STATE:(g + 1) * SSD_STATE] for g in range(SSD_GROUPS)]
    cbs = [_dot_nt(cc, bc) for cc, bc in zip(ccs, bcs)]
    prevs = [sstate_ref[g] for g in range(SSD_GROUPS)]
    y_offs = [_dot(cc, prev.astype(BF16)) for cc, prev in zip(ccs, prevs)]
    new_states = [_dot_tn(bc, xw_b[:, gs]) for bc, gs in zip(bcs, groups)]
    for g in range(SSD_GROUPS):
        sstate_ref[g] = prevs[g] * cdecay_x[:, groups[g]] + new_states[g]
    yd = []
    for h in range(SSD_HEADS):
        diff = acs[:, h:h + 1] - acs_t[h:h + 1, :]
        seg = jnp.exp(jnp.where(causal, diff, -jnp.inf))
        hs = slice(h * SSD_HEAD_DIM, (h + 1) * SSD_HEAD_DIM)
        yd.append(_dot((cbs[h // hpg] * seg).astype(BF16), xdt_b[:, hs]))
    y = jnp.concatenate(yd, axis=-1) + jnp.concatenate(y_offs, axis=-1) * eacs_x + xs * dskip_ref[...]
    y = y * _silu(z_ref[...].astype(F32))
    yn = []
    for g in range(SSD_GROUPS):
        yg = y[:, g * SSD_GROUP_WIDTH:(g + 1) * SSD_GROUP_WIDTH]
        yn.append(yg * lax.rsqrt(jnp.mean(yg * yg, axis=-1, keepdims=True) + EPS))
    yb_ref[...] = (jnp.concatenate(yn, axis=-1) * snorm_ref[...]).astype(yb_ref.dtype)

    cc_, sc_ = ccos_ref[0], csin_ref[0]
    bcos, bsin = bcos_ref[...], bsin_ref[...]
    cos = cc_ * bcos - sc_ * bsin
    sin = sc_ * bcos + cc_ * bsin
    half = RET_QK_DIM // 2
    ko = RET_HEADS * RET_QK_DIM
    go = RET_HEADS * RET_V_DIM
    v_valid = valid((CHUNK, RET_V_DIM))

    def rotated(off):
        x1 = qk_ref[:, off:off + half].astype(F32)
        x2 = qk_ref[:, off + half:off + RET_QK_DIM].astype(F32)
        return jnp.concatenate([x1 * cos - x2 * sin, x1 * sin + x2 * cos], axis=-1)

    qrs, krs, vrs, prevs = [], [], [], []
    for h in range(RET_HEADS):
        qrs.append(rotated(h * RET_QK_DIM).astype(BF16))
        krs.append(rotated(ko + h * RET_QK_DIM) * (RET_QK_DIM ** -0.5))
        vrs.append(jnp.where(v_valid, vg_ref[:, h * RET_V_DIM:(h + 1) * RET_V_DIM].astype(F32), 0.0).astype(BF16))
        prevs.append(rstate_ref[h])
    scores = [_dot_nt(qr, kr.astype(BF16)) for qr, kr in zip(qrs, krs)]
    y_crs = [_dot(qr, prev.astype(BF16)) for qr, prev in zip(qrs, prevs)]
    kvs = [_dot_tn((krs[h] * kdec_ref[h]).astype(BF16), vrs[h]) for h in range(RET_HEADS)]
    for h in range(RET_HEADS):
        log_gamma = math.log(1.0 - 2.0 ** (-5.0 - h))
        rstate_ref[h] = prevs[h] * math.exp(log_gamma * CHUNK) + kvs[h]
    y_ins = [_dot((scores[h] * dmask_ref[h]).astype(BF16), vrs[h]) for h in range(RET_HEADS)]
    yc = []
    for h in range(RET_HEADS):
        gate = vg_ref[:, go + h * RET_V_DIM:go + (h + 1) * RET_V_DIM].astype(F32)
        yh = y_ins[h] + y_crs[h] * qdec_ref[h]
        mu = jnp.mean(yh, axis=-1, keepdims=True)
        yh = yh - mu
        var = jnp.mean(yh * yh, axis=-1, keepdims=True)
        yc.append(yh * lax.rsqrt(var + EPS) * _silu(gate))
    yc_ref[...] = jnp.concatenate(yc, axis=-1).astype(yc_ref.dtype)


def _mixers(proj, xbc, dt_raw, rope, dt_bias, a_log, d_skip, snorm, expand):
    n_rows = proj.shape[0]
    nc = n_rows // CHUNK
    chunk_cos, chunk_sin, base_cos, base_sin = rope

    def col(width, start):
        assert start % width == 0
        return pl.BlockSpec((CHUNK, width), lambda c, s=start // width: (c, s))

    def full(arr):
        return pl.BlockSpec(arr.shape, lambda c: (0,) * arr.ndim)

    row_block = pl.BlockSpec((CHUNK, D_MODEL), lambda c: (c, 0))
    chunk_row = pl.BlockSpec((1, 1, RET_QK_DIM // 2), lambda c: (c, 0, 0))
    weights = (base_cos, base_sin, dt_bias, a_log, d_skip, snorm, expand)
    return pl.pallas_call(
        _mixer_kernel,
        grid=(nc,),
        in_specs=[
            col(D_MODEL, COL_SSD_Z), pl.BlockSpec((CHUNK, SSD_CONV_DIM), lambda c: (c, 0)),
            col(2048, COL_RET_QK), col(2048, COL_RET_VG),
            pl.BlockSpec((CHUNK, DT_LANES), lambda c: (c, 0)), chunk_row, chunk_row,
        ] + [full(w) for w in weights],
        out_specs=[row_block, row_block],
        out_shape=[jax.ShapeDtypeStruct((n_rows, D_MODEL), BF16)] * 2,
        scratch_shapes=[
            pltpu.VMEM((SSD_GROUPS, SSD_STATE, SSD_GROUP_WIDTH), F32),
            pltpu.VMEM((RET_HEADS, RET_QK_DIM, RET_V_DIM), F32),
            pltpu.VMEM((RET_HEADS, CHUNK, CHUNK), F32),
            pltpu.VMEM((RET_HEADS, CHUNK, RET_QK_DIM), F32),
            pltpu.VMEM((RET_HEADS, CHUNK, RET_QK_DIM), F32),
        ],
        compiler_params=pltpu.CompilerParams(
            dimension_semantics=("arbitrary",), vmem_limit_bytes=VMEM_LIMIT),
    )(proj, xbc, proj, proj, dt_raw, chunk_cos, chunk_sin, *weights)


def _sb_kernel(q_ref, k_ref, v_ref, proj_hbm, o_ref, kbuf, vbuf, acc_ref, run_ref, sem):
    i = pl.program_id(0)
    kbuf[i % SB_RING] = k_ref[...]
    vbuf[i % SB_RING] = v_ref[...]
    acc_ref[...] = jnp.zeros_like(acc_ref)
    run_ref[...] = jnp.zeros_like(run_ref)
    rows = lax.broadcasted_iota(jnp.int32, (CHUNK, CHUNK), 0)
    cols = lax.broadcasted_iota(jnp.int32, (CHUNK, CHUNK), 1)
    suffix = (rows >= cols).astype(BF16)
    scale = SB_HEAD_DIM ** -0.5

    def fetch(j):
        far = (i - j) >= SB_RING

        @pl.when(far)
        def _():
            rows_j = pl.ds(pl.multiple_of(j * CHUNK, CHUNK), CHUNK)
            copies = [
                pltpu.make_async_copy(proj_hbm.at[rows_j, pl.ds(COL_SB_K, D_MODEL)], kbuf.at[SB_RING], sem.at[0]),
                pltpu.make_async_copy(proj_hbm.at[rows_j, pl.ds(COL_SB_V, D_MODEL)], vbuf.at[SB_RING], sem.at[1]),
            ]
            for cp in copies:
                cp.start()
            for cp in copies:
                cp.wait()

        return jnp.where(far, SB_RING, j % SB_RING)

    def visit(j, slot, causal, pad):
        mask = None
        if causal:
            mask = cols < rows
        if pad:
            in_seq = (cols + j * CHUNK) >= PAD
            mask = in_seq if mask is None else mask & in_seq
        heads = [slice(h * SB_HEAD_DIM, (h + 1) * SB_HEAD_DIM) for h in range(SB_HEADS)]
        zs = [_dot_nt(q_ref[:, hs], kbuf[slot, :, hs]) * scale for hs in heads]
        csums = []
        for z in zs:
            sp = jnp.maximum(z, 0.0) + jnp.log(1.0 + jnp.exp(-jnp.abs(z)))
            if mask is not None:
                sp = jnp.where(mask, sp, 0.0)
            csums.append(_dot(sp.astype(BF16), suffix))
        run_min = None
        outs = []
        for h, (z, csum) in enumerate(zip(zs, csums)):
            run = run_ref[h]
            w = jnp.exp(z - csum - run)
            if mask is not None:
                w = jnp.where(mask, w, 0.0)
            outs.append(_dot(w.astype(BF16), vbuf[slot, :, heads[h]]))
            run = run + csum[:, 0:1]
            run_ref[h] = run
            run_min = run if run_min is None else jnp.minimum(run_min, run)
        acc_ref[...] += jnp.concatenate(outs, axis=-1)
        return jnp.min(run_min)

    run_min = visit(i, i % SB_RING, causal=True, pad=True)

    def body(carry):
        j, _ = carry
        return j - 1, visit(j, fetch(j), causal=False, pad=False)

    def cond(carry):
        j, run_min = carry
        return (j >= 1) & (run_min < SB_UNDERFLOW)

    j, run_min = lax.while_loop(cond, body, (i - 1, run_min))

    @pl.when((j == 0) & (run_min < SB_UNDERFLOW))
    def _():
        visit(0, fetch(0), causal=False, pad=True)

    o_ref[...] = acc_ref[...].astype(o_ref.dtype)


def _stick_breaking(proj):
    n_rows = proj.shape[0]

    def col(start):
        assert start % D_MODEL == 0
        return pl.BlockSpec((CHUNK, D_MODEL), lambda i, s=start // D_MODEL: (i, s))

    return pl.pallas_call(
        _sb_kernel,
        grid=(n_rows // CHUNK,),
        in_specs=[col(COL_SB_Q), col(COL_SB_K), col(COL_SB_V), pl.BlockSpec(memory_space=pl.ANY)],
        out_specs=pl.BlockSpec((CHUNK, D_MODEL), lambda i: (i, 0)),
        out_shape=jax.ShapeDtypeStruct((n_rows, D_MODEL), BF16),
        scratch_shapes=[
            pltpu.VMEM((SB_RING + 1, CHUNK, D_MODEL), BF16),
            pltpu.VMEM((SB_RING + 1, CHUNK, D_MODEL), BF16),
            pltpu.VMEM((CHUNK, D_MODEL), F32),
            pltpu.VMEM((SB_HEADS, CHUNK, 1), F32),
            pltpu.SemaphoreType.DMA((2,)),
        ],
        compiler_params=pltpu.CompilerParams(
            dimension_semantics=("arbitrary",), vmem_limit_bytes=VMEM_LIMIT),
    )(proj, proj, proj, proj)


def _merge_kernel(ya_ref, yb_ref, yc_ref, yd_ref, ga_ref, gb_ref, gc_ref, gd_ref, h_ref,
                  wb_ref, wo_ref, nw_ref, o_ref):
    merged = None
    branches = ((ya_ref, ga_ref), (yb_ref, gb_ref), (yc_ref, gc_ref), (yd_ref, gd_ref))
    for n, (y_ref, g_ref) in enumerate(branches):
        up = _dot(y_ref[...], wb_ref[n])
        gate = _sigmoid(g_ref[...].astype(F32))
        merged = gate * up if merged is None else merged + gate * up
    mix = _dot(merged.astype(BF16), wo_ref[...])
    o_ref[...] = h_ref[...] + _rmsnorm(mix, nw_ref[...])


def _merge(ya, yb, yc, yd, proj, h, w_branch, w_out, norm_w, layer):
    n_rows = h.shape[0]
    tm = _row_tile(n_rows, 384)
    row_block = pl.BlockSpec((tm, D_MODEL), lambda i: (i, 0))
    return pl.pallas_call(
        _merge_kernel,
        grid=(n_rows // tm,),
        in_specs=[
            row_block, row_block, row_block, row_block,
        ] + [
            pl.BlockSpec((tm, D_MODEL), lambda i, s=COL_GATE // D_MODEL + n: (i, s)) for n in range(N_BRANCH)
        ] + [
            row_block,
            pl.BlockSpec((None, N_BRANCH, D_MODEL, D_MODEL), lambda i: (layer, 0, 0, 0)),
            pl.BlockSpec((None, D_MODEL, D_MODEL), lambda i: (layer, 0, 0)),
            pl.BlockSpec((1, D_MODEL), lambda i: (0, 0)),
        ],
        out_specs=row_block,
        out_shape=jax.ShapeDtypeStruct((n_rows, D_MODEL), F32),
        compiler_params=pltpu.CompilerParams(
            dimension_semantics=("parallel",), vmem_limit_bytes=VMEM_LIMIT),
    )(ya, yb, yc, yd, proj, proj, proj, proj, h, w_branch, w_out, norm_w)


def _ffn_kernel(h_ref, w1_ref, w2_ref, n1_ref, n2_ref, o_ref):
    h = h_ref[...]
    xn = _rmsnorm(h, n1_ref[...]).astype(BF16)
    f = _dot(xn, w1_ref[...])
    act = (_silu(f[:, 0:D_FF]) * f[:, D_FF:2 * D_FF]).astype(BF16)
    o_ref[...] = h + _rmsnorm(_dot(act, w2_ref[...]), n2_ref[...])


def _ffn(h, w1, w2, n1, n2, layer):
    n_rows = h.shape[0]
    tm = _row_tile(n_rows, 384)
    row_block = pl.BlockSpec((tm, D_MODEL), lambda i: (i, 0))
    return pl.pallas_call(
        _ffn_kernel,
        grid=(n_rows // tm,),
        in_specs=[
            row_block,
            pl.BlockSpec((None, D_MODEL, 2 * D_FF), lambda i: (layer, 0, 0)),
            pl.BlockSpec((None, D_FF, D_MODEL), lambda i: (layer, 0, 0)),
            pl.BlockSpec((1, D_MODEL), lambda i: (0, 0)),
            pl.BlockSpec((1, D_MODEL), lambda i: (0, 0)),
        ],
        out_specs=row_block,
        out_shape=jax.ShapeDtypeStruct((n_rows, D_MODEL), F32),
        compiler_params=pltpu.CompilerParams(
            dimension_semantics=("parallel",), vmem_limit_bytes=VMEM_LIMIT),
    )(h, w1, w2, n1, n2)


def _rope_tables(n_chunks):
    half = RET_QK_DIM // 2
    inv = ROPE_BASE ** (-np.arange(half, dtype=np.float64) / half)
    chunk_ang = (np.arange(n_chunks, dtype=np.float64) * CHUNK)[:, None, None] * inv[None, None, :]
    base_ang = np.arange(CHUNK, dtype=np.float64)[:, None] * inv[None, :]
    return tuple(jnp.asarray(t, F32) for t in
                 (np.cos(chunk_ang), np.sin(chunk_ang), np.cos(base_ang), np.sin(base_ang)))


def _head_expand_matrix():
    e = np.zeros((DT_LANES, SSD_INNER), np.float32)
    for h in range(SSD_HEADS):
        e[h, h * SSD_HEAD_DIM:(h + 1) * SSD_HEAD_DIM] = 1.0
    return jnp.asarray(e, BF16)


def _pad_lanes(v):
    return jnp.pad(v.astype(F32), (0, DT_LANES - v.shape[0]))[None, :]


def kernel(x, meta, w_in, conv_a, ssd_conv_w, ssd_conv_b, ssd_dt_bias, ssd_a_log, ssd_d, ssd_norm,
           w_branch, w_out, w_ffn_in, w_ffn_out, norm_mix_pre, norm_mix_post, norm_ffn_pre,
           norm_ffn_post):
    batch, seq, _ = x.shape
    assert batch == 1 and seq % CHUNK == 0
    depth = w_in.shape[0]
    n_rows = seq + CHUNK
    h = jnp.concatenate([jnp.zeros((PAD, D_MODEL), F32), meta.astype(F32), x[0].astype(F32)], axis=0)
    rope = _rope_tables(n_rows // CHUNK)
    expand = _head_expand_matrix()
    dt_end = DT_COL_START + SSD_HEADS
    assert DT_COL_START == INPROJ_A_TILES * D_MODEL
    w_a = w_in[:, :, :DT_COL_START].astype(BF16)
    w_b = w_in[:, :, dt_end:].astype(BF16)
    w_dt = jnp.pad(w_in[:, :, DT_COL_START:dt_end], ((0, 0), (0, 0), (0, DT_LANES - SSD_HEADS))).astype(BF16)
    w_branch_b, w_out_b = w_branch.astype(BF16), w_out.astype(BF16)
    w_ffn_in_b, w_ffn_out_b = w_ffn_in.astype(BF16), w_ffn_out.astype(BF16)
    for l in range(depth):
        proj, dt_raw, ya, xbc = _inproj(h, norm_mix_pre[l][None, :], w_a, w_b, w_dt,
                                        conv_a[l], ssd_conv_w[l], ssd_conv_b[l][None, :], l)
        yb, yc = _mixers(
            proj, xbc, dt_raw, rope, _pad_lanes(ssd_dt_bias[l]), _pad_lanes(ssd_a_log[l]),
            jnp.repeat(ssd_d[l].astype(F32), SSD_HEAD_DIM)[None, :], ssd_norm[l][None, :], expand)
        yd = _stick_breaking(proj)
        h = _merge(ya, yb, yc, yd, proj, h, w_branch_b, w_out_b, norm_mix_post[l][None, :], l)
        h = _ffn(h, w_ffn_in_b, w_ffn_out_b, norm_ffn_pre[l][None, :], norm_ffn_post[l][None, :], l)
    return h[CHUNK:][None].astype(x.dtype)
```

```python
import functools
import math

import numpy as np
import jax
import jax.numpy as jnp
from jax import lax
from jax.experimental import pallas as pl
from jax.experimental.pallas import tpu as pltpu

F32 = jnp.float32
BF16 = jnp.bfloat16

D_MODEL = 1024
N_META = 16
CHUNK = 128
PAD = CHUNK - N_META
EPS = 1e-6

SSD_HEAD_DIM = 64
SSD_HEADS = 16
SSD_INNER = 1024
SSD_GROUPS = 4
SSD_STATE = 128
SSD_CONV_K = 4
SSD_CONV_DIM = SSD_INNER + 2 * SSD_GROUPS * SSD_STATE
SSD_GROUP_WIDTH = SSD_INNER // SSD_GROUPS
CONV_A_K = 3
RET_HEADS = 4
RET_QK_DIM = 256
RET_V_DIM = 256
ROPE_BASE = 10000.0
SB_HEADS = 8
SB_HEAD_DIM = 128
N_BRANCH = 4
D_FF = 2816

COL_CONV = 0
COL_SSD_Z = 3072
COL_SSD_XBC = 4096
COL_RET_QK = 6144
COL_RET_VG = 8192
COL_SB_Q = 10240
COL_SB_K = 11264
COL_SB_V = 12288
COL_GATE = 13312
PROJ_WIDTH = 17408
DT_COL_START = 6144
DT_LANES = 128
W_PREP_ROWS = 128

HIST = 8
SB_UNDERFLOW = 104.0
SB_RING = 4
SB_TOP = 32
VMEM_LIMIT = 56 * 1024 * 1024


def _rmsnorm(x, w):
    return x * lax.rsqrt(jnp.mean(x * x, axis=-1, keepdims=True) + EPS) * w


def _sigmoid(x):
    return 1.0 / (1.0 + jnp.exp(-x))


def _silu(x):
    return x * _sigmoid(x)


def _softplus(x):
    return jnp.maximum(x, 0.0) + jnp.log1p(jnp.exp(-jnp.abs(x)))


def _split3(x):
    hi = x.astype(BF16)
    r = x - hi.astype(F32)
    mid = r.astype(BF16)
    lo = (r - mid.astype(F32)).astype(BF16)
    return hi, mid, lo


def _dot(a, b):
    return jnp.dot(a, b, preferred_element_type=F32)


def _dot_nt(a, b):
    return lax.dot_general(a, b, (((1,), (1,)), ((), ())), preferred_element_type=F32)


def _dot_tn(a, b):
    return lax.dot_general(a, b, (((0,), (0,)), ((), ())), preferred_element_type=F32)


def _row_tile(n_rows, target):
    best = 16
    for t in range(16, min(n_rows, target) + 1, 16):
        if n_rows % t == 0:
            best = t
    return best


def _w_in_prep_kernel(w_ref, main_ref, dt_ref):
    dt_end = DT_COL_START + SSD_HEADS
    main_ref[:, 0:DT_COL_START] = w_ref[:, 0:DT_COL_START].astype(BF16)
    main_ref[:, DT_COL_START:PROJ_WIDTH] = w_ref[:, dt_end:dt_end + PROJ_WIDTH - DT_COL_START].astype(BF16)
    dt_ref[...] = jnp.zeros_like(dt_ref)
    dt_ref[:, 0:SSD_HEADS] = w_ref[:, DT_COL_START:dt_end].astype(BF16)


def _w_in_prep(w_in):
    depth, rows, width = w_in.shape
    assert width == PROJ_WIDTH + SSD_HEADS and rows % W_PREP_ROWS == 0
    return pl.pallas_call(
        _w_in_prep_kernel,
        grid=(depth, rows // W_PREP_ROWS),
        in_specs=[pl.BlockSpec((None, W_PREP_ROWS, width), lambda l, r: (l, r, 0))],
        out_specs=[
            pl.BlockSpec((None, W_PREP_ROWS, PROJ_WIDTH), lambda l, r: (l, r, 0)),
            pl.BlockSpec((None, W_PREP_ROWS, DT_LANES), lambda l, r: (l, r, 0)),
        ],
        out_shape=[
            jax.ShapeDtypeStruct((depth, rows, PROJ_WIDTH), BF16),
            jax.ShapeDtypeStruct((depth, rows, DT_LANES), BF16),
        ],
        compiler_params=pltpu.CompilerParams(
            dimension_semantics=("parallel", "parallel"), vmem_limit_bytes=VMEM_LIMIT),
    )(w_in)


def _inproj_kernel(x_ref, nw_ref, w_ref, wdt_ref, o_ref, dt_ref, xn_ref):
    @pl.when(pl.program_id(1) == 0)
    def _():
        xn = _rmsnorm(x_ref[...], nw_ref[...]).astype(BF16)
        xn_ref[...] = xn
        dt_ref[...] = _dot(xn, wdt_ref[...])

    o_ref[...] = _dot(xn_ref[...], w_ref[...]).astype(o_ref.dtype)


def _inproj(h, norm_w, w_main, w_dt, layer):
    n_rows = h.shape[0]
    tm = _row_tile(n_rows, 1376)
    tn = 1024
    return pl.pallas_call(
        _inproj_kernel,
        grid=(n_rows // tm, PROJ_WIDTH // tn),
        in_specs=[
            pl.BlockSpec((tm, D_MODEL), lambda i, j: (i, 0)),
            pl.BlockSpec((1, D_MODEL), lambda i, j: (0, 0)),
            pl.BlockSpec((None, D_MODEL, tn), lambda i, j: (layer, 0, j)),
            pl.BlockSpec((None, D_MODEL, DT_LANES), lambda i, j: (layer, 0, 0)),
        ],
        out_specs=[
            pl.BlockSpec((tm, tn), lambda i, j: (i, j)),
            pl.BlockSpec((tm, DT_LANES), lambda i, j: (i, 0)),
        ],
        out_shape=[
            jax.ShapeDtypeStruct((n_rows, PROJ_WIDTH), BF16),
            jax.ShapeDtypeStruct((n_rows, DT_LANES), F32),
        ],
        scratch_shapes=[pltpu.VMEM((tm, D_MODEL), BF16)],
        compiler_params=pltpu.CompilerParams(
            dimension_semantics=("parallel", "arbitrary"), vmem_limit_bytes=VMEM_LIMIT),
    )(h, norm_w, w_main, w_dt)


def _mixer_kernel(conv_ref, z_ref, xbc_ref, qk_ref, vg_ref, dt_ref, ccos_ref, csin_ref,
                  bcos_ref, bsin_ref,
                  conva_ref, sconvw_ref, sconvb_ref, dtb_ref, alog_ref, dskip_ref, snorm_ref,
                  expand_ref,
                  ya_ref, yb_ref, yc_ref,
                  uhist_ref, xhist_ref, sstate_ref, rstate_ref, dmask_ref, kdec_ref, qdec_ref):
    c = pl.program_id(0)
    li = lax.broadcasted_iota(jnp.int32, (CHUNK, CHUNK), 0)
    si = lax.broadcasted_iota(jnp.int32, (CHUNK, CHUNK), 1)
    causal = li >= si

    @pl.when(c == 0)
    def _():
        uhist_ref[0:HIST, :] = jnp.zeros((HIST, D_MODEL), F32)
        xhist_ref[0:HIST, :] = jnp.zeros((HIST, SSD_CONV_DIM), F32)
        sstate_ref[...] = jnp.zeros_like(sstate_ref)
        rstate_ref[...] = jnp.zeros_like(rstate_ref)
        rel = (li - si).astype(F32)
        row_w = lax.broadcasted_iota(jnp.int32, (CHUNK, RET_QK_DIM), 0).astype(F32)
        for h in range(RET_HEADS):
            log_gamma = math.log(1.0 - 2.0 ** (-5.0 - h))
            dmask_ref[h] = jnp.where(causal, jnp.exp(log_gamma * jnp.maximum(rel, 0.0)), 0.0)
            kdec_ref[h] = jnp.exp(log_gamma * (CHUNK - 1.0 - row_w))
            qdec_ref[h] = jnp.exp(log_gamma * (row_w + 1.0))

    def valid(shape):
        return (lax.broadcasted_iota(jnp.int32, shape, 0) + c * CHUNK) >= PAD

    b_gate = conv_ref[:, 0:D_MODEL].astype(F32)
    c_gate = conv_ref[:, D_MODEL:2 * D_MODEL].astype(F32)
    xa = conv_ref[:, 2 * D_MODEL:3 * D_MODEL].astype(F32)
    u = jnp.where(valid((CHUNK, D_MODEL)), c_gate * xa, 0.0)
    uhist_ref[HIST:HIST + CHUNK, :] = u
    conv = u * conva_ref[CONV_A_K - 1:CONV_A_K, :]
    for i in range(CONV_A_K - 1):
        off = HIST - (CONV_A_K - 1) + i
        conv = conv + uhist_ref[off:off + CHUNK, :] * conva_ref[i:i + 1, :]
    ya_ref[...] = (b_gate * conv).astype(ya_ref.dtype)
    uhist_ref[0:HIST, :] = uhist_ref[CHUNK:CHUNK + HIST, :]

    xin = jnp.where(valid((CHUNK, SSD_CONV_DIM)), xbc_ref[...].astype(F32), 0.0)
    xhist_ref[HIST:HIST + CHUNK, :] = xin
    xc = xin * sconvw_ref[SSD_CONV_K - 1:SSD_CONV_K, :] + sconvb_ref[...]
    for i in range(SSD_CONV_K - 1):
        off = HIST - (SSD_CONV_K - 1) + i
        xc = xc + xhist_ref[off:off + CHUNK, :] * sconvw_ref[i:i + 1, :]
    xhist_ref[0:HIST, :] = xhist_ref[CHUNK:CHUNK + HIST, :]
    xc = _silu(xc)
    xs = jnp.where(valid((CHUNK, SSD_INNER)), xc[:, 0:SSD_INNER], 0.0)
    bm = xc[:, SSD_INNER:SSD_INNER + SSD_GROUPS * SSD_STATE].astype(BF16)
    cm = xc[:, SSD_INNER + SSD_GROUPS * SSD_STATE:SSD_CONV_DIM].astype(BF16)

    dtv = _softplus(dt_ref[...] + dtb_ref[...])
    a = -jnp.exp(alog_ref[...]) * dtv
    tri = causal.astype(BF16)
    acs = sum(_dot(tri, part) for part in _split3(a))
    acs_t = acs.T
    acs_last = acs[CHUNK - 1:CHUNK, :]
    expand = expand_ref[...]

    def expand_heads(v):
        return sum(_dot(part, expand) for part in _split3(v))

    dt_x = expand_heads(dtv)
    eacs_x = jnp.exp(expand_heads(acs))
    dstate_x = jnp.exp(expand_heads(acs_last - acs))
    cdecay_x = eacs_x[CHUNK - 1:CHUNK, :]
    xdt = xs * dt_x
    xdt_b = xdt.astype(BF16)
    xw_b = (xdt * dstate_x).astype(BF16)

    hpg = SSD_HEADS // SSD_GROUPS
    groups = [slice(g * SSD_GROUP_WIDTH, (g + 1) * SSD_GROUP_WIDTH) for g in range(SSD_GROUPS)]
    ccs = [cm[:, g * SSD_STATE:(g + 1) * SSD_STATE] for g in range(SSD_GROUPS)]
    bcs = [bm[:, g * SSD_STATE:(g + 1) * SSD_STATE] for g in range(SSD_GROUPS)]
    cbs = [_dot_nt(cc, bc) for cc, bc in zip(ccs, bcs)]
    prevs = [sstate_ref[g] for g in range(SSD_GROUPS)]
    y_offs = [_dot(cc, prev.astype(BF16)) for cc, prev in zip(ccs, prevs)]
    new_states = [_dot_tn(bc, xw_b[:, gs]) for bc, gs in zip(bcs, groups)]
    for g in range(SSD_GROUPS):
        sstate_ref[g] = prevs[g] * cdecay_x[:, groups[g]] + new_states[g]
    yd = []
    for h in range(SSD_HEADS):
        diff = acs[:, h:h + 1] - acs_t[h:h + 1, :]
        seg = jnp.exp(jnp.where(causal, diff, -jnp.inf))
        hs = slice(h * SSD_HEAD_DIM, (h + 1) * SSD_HEAD_DIM)
        yd.append(_dot((cbs[h // hpg] * seg).astype(BF16), xdt_b[:, hs]))
    y = jnp.concatenate(yd, axis=-1) + jnp.concatenate(y_offs, axis=-1) * eacs_x + xs * dskip_ref[...]
    y = y * _silu(z_ref[...].astype(F32))
    yn = []
    for g in range(SSD_GROUPS):
        yg = y[:, g * SSD_GROUP_WIDTH:(g + 1) * SSD_GROUP_WIDTH]
        yn.append(yg * lax.rsqrt(jnp.mean(yg * yg, axis=-1, keepdims=True) + EPS))
    yb_ref[...] = (jnp.concatenate(yn, axis=-1) * snorm_ref[...]).astype(yb_ref.dtype)

    cc_, sc_ = ccos_ref[0], csin_ref[0]
    bcos, bsin = bcos_ref[...], bsin_ref[...]
    cos = cc_ * bcos - sc_ * bsin
    sin = sc_ * bcos + cc_ * bsin
    half = RET_QK_DIM // 2
    ko = RET_HEADS * RET_QK_DIM
    go = RET_HEADS * RET_V_DIM
    v_valid = valid((CHUNK, RET_V_DIM))

    def rotated(off):
        x1 = qk_ref[:, off:off + half].astype(F32)
        x2 = qk_ref[:, off + half:off + RET_QK_DIM].astype(F32)
        return jnp.concatenate([x1 * cos - x2 * sin, x1 * sin + x2 * cos], axis=-1)

    qrs, krs, vrs, prevs = [], [], [], []
    for h in range(RET_HEADS):
        qrs.append(rotated(h * RET_QK_DIM).astype(BF16))
        krs.append(rotated(ko + h * RET_QK_DIM) * (RET_QK_DIM ** -0.5))
        vrs.append(jnp.where(v_valid, vg_ref[:, h * RET_V_DIM:(h + 1) * RET_V_DIM].astype(F32), 0.0).astype(BF16))
        prevs.append(rstate_ref[h])
    scores = [_dot_nt(qr, kr.astype(BF16)) for qr, kr in zip(qrs, krs)]
    y_crs = [_dot(qr, prev.astype(BF16)) for qr, prev in zip(qrs, prevs)]
    kvs = [_dot_tn((krs[h] * kdec_ref[h]).astype(BF16), vrs[h]) for h in range(RET_HEADS)]
    for h in range(RET_HEADS):
        log_gamma = math.log(1.0 - 2.0 ** (-5.0 - h))
        rstate_ref[h] = prevs[h] * math.exp(log_gamma * CHUNK) + kvs[h]
    y_ins = [_dot((scores[h] * dmask_ref[h]).astype(BF16), vrs[h]) for h in range(RET_HEADS)]
    yc = []
    for h in range(RET_HEADS):
        gate = vg_ref[:, go + h * RET_V_DIM:go + (h + 1) * RET_V_DIM].astype(F32)
        yh = y_ins[h] + y_crs[h] * qdec_ref[h]
        mu = jnp.mean(yh, axis=-1, keepdims=True)
        yh = yh - mu
        var = jnp.mean(yh * yh, axis=-1, keepdims=True)
        yc.append(yh * lax.rsqrt(var + EPS) * _silu(gate))
    yc_ref[...] = jnp.concatenate(yc, axis=-1).astype(yc_ref.dtype)


def _mixers(proj, dt_raw, rope, conv_a, sconv_w, sconv_b, dt_bias, a_log, d_skip, snorm, expand):
    n_rows = proj.shape[0]
    nc = n_rows // CHUNK
    chunk_cos, chunk_sin, base_cos, base_sin = rope

    def col(width, start):
        assert start % width == 0
        return pl.BlockSpec((CHUNK, width), lambda c, s=start // width: (c, s))

    def full(arr):
        return pl.BlockSpec(arr.shape, lambda c: (0,) * arr.ndim)

    row_block = pl.BlockSpec((CHUNK, D_MODEL), lambda c: (c, 0))
    chunk_row = pl.BlockSpec((1, 1, RET_QK_DIM // 2), lambda c: (c, 0, 0))
    weights = (base_cos, base_sin, conv_a, sconv_w, sconv_b, dt_bias, a_log, d_skip, snorm, expand)
    return pl.pallas_call(
        _mixer_kernel,
        grid=(nc,),
        in_specs=[
            col(3 * D_MODEL, COL_CONV), col(D_MODEL, COL_SSD_Z), col(SSD_CONV_DIM, COL_SSD_XBC),
            col(2048, COL_RET_QK), col(2048, COL_RET_VG),
            pl.BlockSpec((CHUNK, DT_LANES), lambda c: (c, 0)), chunk_row, chunk_row,
        ] + [full(w) for w in weights],
        out_specs=[row_block, row_block, row_block],
        out_shape=[jax.ShapeDtypeStruct((n_rows, D_MODEL), BF16)] * 3,
        scratch_shapes=[
            pltpu.VMEM((HIST + CHUNK, D_MODEL), F32),
            pltpu.VMEM((HIST + CHUNK, SSD_CONV_DIM), F32),
            pltpu.VMEM((SSD_GROUPS, SSD_STATE, SSD_GROUP_WIDTH), F32),
            pltpu.VMEM((RET_HEADS, RET_QK_DIM, RET_V_DIM), F32),
            pltpu.VMEM((RET_HEADS, CHUNK, CHUNK), F32),
            pltpu.VMEM((RET_HEADS, CHUNK, RET_QK_DIM), F32),
            pltpu.VMEM((RET_HEADS, CHUNK, RET_QK_DIM), F32),
        ],
        compiler_params=pltpu.CompilerParams(
            dimension_semantics=("arbitrary",), vmem_limit_bytes=VMEM_LIMIT),
    )(proj, proj, proj, proj, proj, dt_raw, chunk_cos, chunk_sin, *weights)


def _sb_kernel(q_ref, k_ref, v_ref, proj_hbm, o_ref, kbuf, vbuf, acc_ref, run_ref, sem):
    i = pl.program_id(0)
    kbuf[i % SB_RING] = k_ref[...]
    vbuf[i % SB_RING] = v_ref[...]
    acc_ref[...] = jnp.zeros_like(acc_ref)
    run_ref[...] = jnp.zeros_like(run_ref)
    scale = SB_HEAD_DIM ** -0.5
    heads = [slice(h * SB_HEAD_DIM, (h + 1) * SB_HEAD_DIM) for h in range(SB_HEADS)]
    sfx_r = lax.broadcasted_iota(jnp.int32, (CHUNK, CHUNK), 0)
    sfx_c = lax.broadcasted_iota(jnp.int32, (CHUNK, CHUNK), 1)
    suffix = (sfx_r >= sfx_c).astype(BF16)

    def fetch(j):
        far = (i - j) >= SB_RING

        @pl.when(far)
        def _():
            rows_j = pl.ds(pl.multiple_of(j * CHUNK, CHUNK), CHUNK)
            copies = [
                pltpu.make_async_copy(proj_hbm.at[rows_j, pl.ds(COL_SB_K, D_MODEL)], kbuf.at[SB_RING], sem.at[0]),
                pltpu.make_async_copy(proj_hbm.at[rows_j, pl.ds(COL_SB_V, D_MODEL)], vbuf.at[SB_RING], sem.at[1]),
            ]
            for cp in copies:
                cp.start()
            for cp in copies:
                cp.wait()

        return jnp.where(far, SB_RING, j % SB_RING)

    def visit(j, slot, causal, pad, n):
        rows = lax.broadcasted_iota(jnp.int32, (n, CHUNK), 0)
        cols = lax.broadcasted_iota(jnp.int32, (n, CHUNK), 1)
        mask = None
        if causal:
            mask = cols < rows
        if pad:
            in_seq = (cols + j * CHUNK) >= PAD
            mask = in_seq if mask is None else mask & in_seq
        zs = [_dot_nt(q_ref[0:n, hs], kbuf[slot, :, hs]) * scale for hs in heads]
        csums = []
        for z in zs:
            sp = jnp.maximum(z, 0.0) + jnp.log(1.0 + jnp.exp(-jnp.abs(z)))
            if mask is not None:
                sp = jnp.where(mask, sp, 0.0)
            csums.append(_dot(sp.astype(BF16), suffix))
        run_min = None
        outs = []
        for h, (z, csum) in enumerate(zip(zs, csums)):
            run = run_ref[h, 0:n, :]
            w = jnp.exp(z - csum - run)
            if mask is not None:
                w = jnp.where(mask, w, 0.0)
            outs.append(_dot(w.astype(BF16), vbuf[slot, :, heads[h]]))
            run = run + csum[:, 0:1]
            run_ref[h, 0:n, :] = run
            run_min = run if run_min is None else jnp.minimum(run_min, run)
        acc_ref[0:n, :] += jnp.concatenate(outs, axis=-1)
        top = jnp.min(run_min[0:SB_TOP])
        rest = jnp.min(run_min[SB_TOP:n]) if n > SB_TOP else None
        return top, rest

    def step(j, top, rest, pad):
        slot = fetch(j)

        def top_rows():
            return visit(j, slot, False, pad, SB_TOP)[0], rest

        def all_rows():
            return visit(j, slot, False, pad, CHUNK)

        return lax.cond(rest >= SB_UNDERFLOW, top_rows, all_rows)

    top, rest = visit(i, i % SB_RING, True, True, CHUNK)

    def body(carry):
        j, top, rest = carry
        top, rest = step(j, top, rest, False)
        return j - 1, top, rest

    def cond(carry):
        j, top, rest = carry
        return (j >= 1) & (jnp.minimum(top, rest) < SB_UNDERFLOW)

    j, top, rest = lax.while_loop(cond, body, (i - 1, top, rest))

    @pl.when((j == 0) & (jnp.minimum(top, rest) < SB_UNDERFLOW))
    def _():
        step(0, top, rest, True)

    o_ref[...] = acc_ref[...].astype(o_ref.dtype)


def _stick_breaking(proj):
    n_rows = proj.shape[0]

    def col(start):
        assert start % D_MODEL == 0
        return pl.BlockSpec((CHUNK, D_MODEL), lambda i, s=start // D_MODEL: (i, s))

    return pl.pallas_call(
        _sb_kernel,
        grid=(n_rows // CHUNK,),
        in_specs=[col(COL_SB_Q), col(COL_SB_K), col(COL_SB_V), pl.BlockSpec(memory_space=pl.ANY)],
        out_specs=pl.BlockSpec((CHUNK, D_MODEL), lambda i: (i, 0)),
        out_shape=jax.ShapeDtypeStruct((n_rows, D_MODEL), BF16),
        scratch_shapes=[
            pltpu.VMEM((SB_RING + 1, CHUNK, D_MODEL), BF16),
            pltpu.VMEM((SB_RING + 1, CHUNK, D_MODEL), BF16),
            pltpu.VMEM((CHUNK, D_MODEL), F32),
            pltpu.VMEM((SB_HEADS, CHUNK, 1), F32),
            pltpu.SemaphoreType.DMA((2,)),
        ],
        compiler_params=pltpu.CompilerParams(
            dimension_semantics=("arbitrary",), vmem_limit_bytes=VMEM_LIMIT),
    )(proj, proj, proj, proj)


def _merge_kernel(ya_ref, yb_ref, yc_ref, yd_ref, ga_ref, gb_ref, gc_ref, gd_ref, h_ref,
                  wb_ref, wo_ref, nw_ref, o_ref):
    merged = None
    branches = ((ya_ref, ga_ref), (yb_ref, gb_ref), (yc_ref, gc_ref), (yd_ref, gd_ref))
    for n, (y_ref, g_ref) in enumerate(branches):
        up = _dot(y_ref[...], wb_ref[n])
        gate = _sigmoid(g_ref[...].astype(F32))
        merged = gate * up if merged is None else merged + gate * up
    mix = _dot(merged.astype(BF16), wo_ref[...])
    o_ref[...] = h_ref[...] + _rmsnorm(mix, nw_ref[...])


def _merge(ya, yb, yc, yd, proj, h, w_branch, w_out, norm_w, layer):
    n_rows = h.shape[0]
    tm = _row_tile(n_rows, 384)
    row_block = pl.BlockSpec((tm, D_MODEL), lambda i: (i, 0))
    return pl.pallas_call(
        _merge_kernel,
        grid=(n_rows // tm,),
        in_specs=[
            row_block, row_block, row_block, row_block,
        ] + [
            pl.BlockSpec((tm, D_MODEL), lambda i, s=COL_GATE // D_MODEL + n: (i, s)) for n in range(N_BRANCH)
        ] + [
            row_block,
            pl.BlockSpec((None, N_BRANCH, D_MODEL, D_MODEL), lambda i: (layer, 0, 0, 0)),
            pl.BlockSpec((None, D_MODEL, D_MODEL), lambda i: (layer, 0, 0)),
            pl.BlockSpec((1, D_MODEL), lambda i: (0, 0)),
        ],
        out_specs=row_block,
        out_shape=jax.ShapeDtypeStruct((n_rows, D_MODEL), F32),
        compiler_params=pltpu.CompilerParams(
            dimension_semantics=("parallel",), vmem_limit_bytes=VMEM_LIMIT),
    )(ya, yb, yc, yd, proj, proj, proj, proj, h, w_branch, w_out, norm_w)


def _ffn_kernel(h_ref, w1_ref, w2_ref, n1_ref, n2_ref, o_ref):
    h = h_ref[...]
    xn = _rmsnorm(h, n1_ref[...]).astype(BF16)
    f = _dot(xn, w1_ref[...])
    act = (_silu(f[:, 0:D_FF]) * f[:, D_FF:2 * D_FF]).astype(BF16)
    o_ref[...] = h + _rmsnorm(_dot(act, w2_ref[...]), n2_ref[...])


def _ffn(h, w1, w2, n1, n2, layer):
    n_rows = h.shape[0]
    tm = _row_tile(n_rows, 384)
    row_block = pl.BlockSpec((tm, D_MODEL), lambda i: (i, 0))
    return pl.pallas_call(
        _ffn_kernel,
        grid=(n_rows // tm,),
        in_specs=[
            row_block,
            pl.BlockSpec((None, D_MODEL, 2 * D_FF), lambda i: (layer, 0, 0)),
            pl.BlockSpec((None, D_FF, D_MODEL), lambda i: (layer, 0, 0)),
            pl.BlockSpec((1, D_MODEL), lambda i: (0, 0)),
            pl.BlockSpec((1, D_MODEL), lambda i: (0, 0)),
        ],
        out_specs=row_block,
        out_shape=jax.ShapeDtypeStruct((n_rows, D_MODEL), F32),
        compiler_params=pltpu.CompilerParams(
            dimension_semantics=("parallel",), vmem_limit_bytes=VMEM_LIMIT),
    )(h, w1, w2, n1, n2)


def _rope_tables(n_chunks):
    half = RET_QK_DIM // 2
    inv = ROPE_BASE ** (-np.arange(half, dtype=np.float64) / half)
    chunk_ang = (np.arange(n_chunks, dtype=np.float64) * CHUNK)[:, None, None] * inv[None, None, :]
    base_ang = np.arange(CHUNK, dtype=np.float64)[:, None] * inv[None, :]
    return tuple(jnp.asarray(t, F32) for t in
                 (np.cos(chunk_ang), np.sin(chunk_ang), np.cos(base_ang), np.sin(base_ang)))


def _head_expand_matrix():
    e = np.zeros((DT_LANES, SSD_INNER), np.float32)
    for h in range(SSD_HEADS):
        e[h, h * SSD_HEAD_DIM:(h + 1) * SSD_HEAD_DIM] = 1.0
    return jnp.asarray(e, BF16)


def _pad_lanes(v):
    return jnp.pad(v.astype(F32), (0, DT_LANES - v.shape[0]))[None, :]


def kernel(x, meta, w_in, conv_a, ssd_conv_w, ssd_conv_b, ssd_dt_bias, ssd_a_log, ssd_d, ssd_norm,
           w_branch, w_out, w_ffn_in, w_ffn_out, norm_mix_pre, norm_mix_post, norm_ffn_pre,
           norm_ffn_post):
    batch, seq, _ = x.shape
    assert batch == 1 and seq % CHUNK == 0
    depth = w_in.shape[0]
    n_rows = seq + CHUNK
    h = jnp.concatenate([jnp.zeros((PAD, D_MODEL), F32), meta.astype(F32), x[0].astype(F32)], axis=0)
    rope = _rope_tables(n_rows // CHUNK)
    expand = _head_expand_matrix()
    w_main, w_dt = _w_in_prep(w_in)
    w_branch_b, w_out_b = w_branch.astype(BF16), w_out.astype(BF16)
    w_ffn_in_b, w_ffn_out_b = w_ffn_in.astype(BF16), w_ffn_out.astype(BF16)
    for l in range(depth):
        proj, dt_raw = _inproj(h, norm_mix_pre[l][None, :], w_main, w_dt, l)
        ya, yb, yc = _mixers(
            proj, dt_raw, rope, conv_a[l], ssd_conv_w[l], ssd_conv_b[l][None, :],
            _pad_lanes(ssd_dt_bias[l]), _pad_lanes(ssd_a_log[l]),
            jnp.repeat(ssd_d[l].astype(F32), SSD_HEAD_DIM)[None, :], ssd_norm[l][None, :], expand)
        yd = _stick_breaking(proj)
        h = _merge(ya, yb, yc, yd, proj, h, w_branch_b, w_out_b, norm_mix_post[l][None, :], l)
        h = _ffn(h, w_ffn_in_b, w_ffn_out_b, norm_ffn_pre[l][None, :], norm_ffn_post[l][None, :], l)
    return h[CHUNK:][None].astype(x.dtype)
```

```python
import functools
import math

import numpy as np
import jax
import jax.numpy as jnp
from jax import lax
from jax.experimental import pallas as pl
from jax.experimental.pallas import tpu as pltpu

F32 = jnp.float32
BF16 = jnp.bfloat16

D_MODEL = 1024
N_META = 16
CHUNK = 128
PAD = CHUNK - N_META
EPS = 1e-6

SSD_HEAD_DIM = 64
SSD_HEADS = 16
SSD_INNER = 1024
SSD_GROUPS = 4
SSD_STATE = 128
SSD_CONV_K = 4
SSD_CONV_DIM = SSD_INNER + 2 * SSD_GROUPS * SSD_STATE
SSD_GROUP_WIDTH = SSD_INNER // SSD_GROUPS
CONV_A_K = 3
RET_HEADS = 4
RET_QK_DIM = 256
RET_V_DIM = 256
ROPE_BASE = 10000.0
SB_HEADS = 8
SB_HEAD_DIM = 128
N_BRANCH = 4
D_FF = 2816

COL_CONV = 0
COL_SSD_Z = 3072
COL_SSD_XBC = 4096
COL_RET_QK = 6144
COL_RET_VG = 8192
COL_SB_Q = 10240
COL_SB_K = 11264
COL_SB_V = 12288
COL_GATE = 13312
PROJ_WIDTH = 17408
DT_COL_START = 6144
DT_LANES = 128
CONV_STRIP = 256

HIST = 8
SB_UNDERFLOW = 104.0
SB_RING = 4
SB_TOP = 32
VMEM_LIMIT = 56 * 1024 * 1024


def _rmsnorm(x, w):
    return x * lax.rsqrt(jnp.mean(x * x, axis=-1, keepdims=True) + EPS) * w


def _sigmoid(x):
    return 1.0 / (1.0 + jnp.exp(-x))


def _silu(x):
    return x * _sigmoid(x)


def _softplus(x):
    return jnp.maximum(x, 0.0) + jnp.log1p(jnp.exp(-jnp.abs(x)))


def _split3(x):
    hi = x.astype(BF16)
    r = x - hi.astype(F32)
    mid = r.astype(BF16)
    lo = (r - mid.astype(F32)).astype(BF16)
    return hi, mid, lo


def _dot(a, b):
    return jnp.dot(a, b, preferred_element_type=F32)


def _dot_nt(a, b):
    return lax.dot_general(a, b, (((1,), (1,)), ((), ())), preferred_element_type=F32)


def _dot_tn(a, b):
    return lax.dot_general(a, b, (((0,), (0,)), ((), ())), preferred_element_type=F32)


def _row_tile(n_rows, target):
    best = 16
    for t in range(16, min(n_rows, target) + 1, 16):
        if n_rows % t == 0:
            best = t
    return best


def _inproj_kernel(x_ref, nw_ref, w_ref, wdt_ref, o_ref, dt_ref, xn_ref):
    @pl.when(pl.program_id(1) == 0)
    def _():
        xn = _rmsnorm(x_ref[...], nw_ref[...]).astype(BF16)
        xn_ref[...] = xn
        dt_ref[...] = _dot(xn, wdt_ref[...])

    o_ref[...] = _dot(xn_ref[...], w_ref[...]).astype(o_ref.dtype)


def _inproj(h, norm_w, w_main, w_dt, layer):
    n_rows = h.shape[0]
    tm = _row_tile(n_rows, 1376)
    tn = 1024
    return pl.pallas_call(
        _inproj_kernel,
        grid=(n_rows // tm, PROJ_WIDTH // tn),
        in_specs=[
            pl.BlockSpec((tm, D_MODEL), lambda i, j: (i, 0)),
            pl.BlockSpec((1, D_MODEL), lambda i, j: (0, 0)),
            pl.BlockSpec((None, D_MODEL, tn), lambda i, j: (layer, 0, j)),
            pl.BlockSpec((None, D_MODEL, DT_LANES), lambda i, j: (layer, 0, 0)),
        ],
        out_specs=[
            pl.BlockSpec((tm, tn), lambda i, j: (i, j)),
            pl.BlockSpec((tm, DT_LANES), lambda i, j: (i, 0)),
        ],
        out_shape=[
            jax.ShapeDtypeStruct((n_rows, PROJ_WIDTH), BF16),
            jax.ShapeDtypeStruct((n_rows, DT_LANES), F32),
        ],
        scratch_shapes=[pltpu.VMEM((tm, D_MODEL), BF16)],
        compiler_params=pltpu.CompilerParams(
            dimension_semantics=("parallel", "arbitrary"), vmem_limit_bytes=VMEM_LIMIT),
    )(h, norm_w, w_main, w_dt)


def _mixer_kernel(conv_ref, z_ref, xbc_ref, qk_ref, vg_ref, dt_ref, ccos_ref, csin_ref,
                  bcos_ref, bsin_ref,
                  conva_ref, sconvw_ref, sconvb_ref, dtb_ref, alog_ref, dskip_ref, snorm_ref,
                  expand_ref,
                  ya_ref, yb_ref, yc_ref,
                  uhist_ref, xhist_ref, xs_ref, bc_ref, sstate_ref, rstate_ref, dmask_ref, kdec_ref,
                  qdec_ref):
    c = pl.program_id(0)
    li = lax.broadcasted_iota(jnp.int32, (CHUNK, CHUNK), 0)
    si = lax.broadcasted_iota(jnp.int32, (CHUNK, CHUNK), 1)
    causal = li >= si

    @pl.when(c == 0)
    def _():
        uhist_ref[0:HIST, :] = jnp.zeros((HIST, D_MODEL), F32)
        xhist_ref[0:HIST, :] = jnp.zeros((HIST, SSD_CONV_DIM), F32)
        sstate_ref[...] = jnp.zeros_like(sstate_ref)
        rstate_ref[...] = jnp.zeros_like(rstate_ref)
        rel = (li - si).astype(F32)
        row_w = lax.broadcasted_iota(jnp.int32, (CHUNK, RET_QK_DIM), 0).astype(F32)
        for h in range(RET_HEADS):
            log_gamma = math.log(1.0 - 2.0 ** (-5.0 - h))
            dmask_ref[h] = jnp.where(causal, jnp.exp(log_gamma * jnp.maximum(rel, 0.0)), 0.0)
            kdec_ref[h] = jnp.exp(log_gamma * (CHUNK - 1.0 - row_w))
            qdec_ref[h] = jnp.exp(log_gamma * (row_w + 1.0))

    def valid(shape):
        return (lax.broadcasted_iota(jnp.int32, shape, 0) + c * CHUNK) >= PAD

    strip_valid = valid((CHUNK, CONV_STRIP))

    def causal_conv(u, hist_ref, cs, taps_ref):
        k_taps = taps_ref.shape[0]
        hist_ref[HIST:HIST + CHUNK, cs] = u
        out = u * taps_ref[k_taps - 1:k_taps, cs]
        for t in range(k_taps - 1):
            off = HIST - (k_taps - 1) + t
            out = out + hist_ref[off:off + CHUNK, cs] * taps_ref[t:t + 1, cs]
        hist_ref[0:HIST, cs] = hist_ref[CHUNK:CHUNK + HIST, cs]
        return out

    for n in range(D_MODEL // CONV_STRIP):
        cs = slice(n * CONV_STRIP, (n + 1) * CONV_STRIP)
        b_gate = conv_ref[:, cs].astype(F32)
        c_gate = conv_ref[:, D_MODEL + cs.start:D_MODEL + cs.stop].astype(F32)
        xa = conv_ref[:, 2 * D_MODEL + cs.start:2 * D_MODEL + cs.stop].astype(F32)
        u = jnp.where(strip_valid, c_gate * xa, 0.0)
        ya_ref[:, cs] = (b_gate * causal_conv(u, uhist_ref, cs, conva_ref)).astype(ya_ref.dtype)

    for n in range(SSD_CONV_DIM // CONV_STRIP):
        cs = slice(n * CONV_STRIP, (n + 1) * CONV_STRIP)
        xin = jnp.where(strip_valid, xbc_ref[:, cs].astype(F32), 0.0)
        xc = _silu(causal_conv(xin, xhist_ref, cs, sconvw_ref) + sconvb_ref[:, cs])
        if cs.stop <= SSD_INNER:
            xs_ref[:, cs] = jnp.where(strip_valid, xc, 0.0)
        else:
            bc_ref[:, cs.start - SSD_INNER:cs.stop - SSD_INNER] = xc.astype(BF16)
    xs = xs_ref[...]
    bm = bc_ref[:, 0:SSD_GROUPS * SSD_STATE]
    cm = bc_ref[:, SSD_GROUPS * SSD_STATE:2 * SSD_GROUPS * SSD_STATE]

    dtv = _softplus(dt_ref[...] + dtb_ref[...])
    a = -jnp.exp(alog_ref[...]) * dtv
    tri = causal.astype(BF16)
    acs = sum(_dot(tri, part) for part in _split3(a))
    acs_t = acs.T
    acs_last = acs[CHUNK - 1:CHUNK, :]
    expand = expand_ref[...]

    def expand_heads(v):
        return sum(_dot(part, expand) for part in _split3(v))

    dt_x = expand_heads(dtv)
    eacs_x = jnp.exp(expand_heads(acs))
    dstate_x = jnp.exp(expand_heads(acs_last - acs))
    cdecay_x = eacs_x[CHUNK - 1:CHUNK, :]
    xdt = xs * dt_x
    xdt_b = xdt.astype(BF16)
    xw_b = (xdt * dstate_x).astype(BF16)

    hpg = SSD_HEADS // SSD_GROUPS
    groups = [slice(g * SSD_GROUP_WIDTH, (g + 1) * SSD_GROUP_WIDTH) for g in range(SSD_GROUPS)]
    ccs = [cm[:, g * SSD_STATE:(g + 1) * SSD_STATE] for g in range(SSD_GROUPS)]
    bcs = [bm[:, g * SSD_STATE:(g + 1) * SSD_STATE] for g in range(SSD_GROUPS)]
    cbs = [_dot_nt(cc, bc) for cc, bc in zip(ccs, bcs)]
    prevs = [sstate_ref[g] for g in range(SSD_GROUPS)]
    y_offs = [_dot(cc, prev.astype(BF16)) for cc, prev in zip(ccs, prevs)]
    new_states = [_dot_tn(bc, xw_b[:, gs]) for bc, gs in zip(bcs, groups)]
    for g in range(SSD_GROUPS):
        sstate_ref[g] = prevs[g] * cdecay_x[:, groups[g]] + new_states[g]
    yd = []
    for h in range(SSD_HEADS):
        diff = acs[:, h:h + 1] - acs_t[h:h + 1, :]
        seg = jnp.exp(jnp.where(causal, diff, -jnp.inf))
        hs = slice(h * SSD_HEAD_DIM, (h + 1) * SSD_HEAD_DIM)
        yd.append(_dot((cbs[h // hpg] * seg).astype(BF16), xdt_b[:, hs]))
    y = jnp.concatenate(yd, axis=-1) + jnp.concatenate(y_offs, axis=-1) * eacs_x + xs * dskip_ref[...]
    y = y * _silu(z_ref[...].astype(F32))
    yn = []
    for g in range(SSD_GROUPS):
        yg = y[:, g * SSD_GROUP_WIDTH:(g + 1) * SSD_GROUP_WIDTH]
        yn.append(yg * lax.rsqrt(jnp.mean(yg * yg, axis=-1, keepdims=True) + EPS))
    yb_ref[...] = (jnp.concatenate(yn, axis=-1) * snorm_ref[...]).astype(yb_ref.dtype)

    cc_, sc_ = ccos_ref[0], csin_ref[0]
    bcos, bsin = bcos_ref[...], bsin_ref[...]
    cos = cc_ * bcos - sc_ * bsin
    sin = sc_ * bcos + cc_ * bsin
    half = RET_QK_DIM // 2
    ko = RET_HEADS * RET_QK_DIM
    go = RET_HEADS * RET_V_DIM
    v_valid = valid((CHUNK, RET_V_DIM))

    def rotated(off):
        x1 = qk_ref[:, off:off + half].astype(F32)
        x2 = qk_ref[:, off + half:off + RET_QK_DIM].astype(F32)
        return jnp.concatenate([x1 * cos - x2 * sin, x1 * sin + x2 * cos], axis=-1)

    qrs, krs, vrs, prevs = [], [], [], []
    for h in range(RET_HEADS):
        qrs.append(rotated(h * RET_QK_DIM).astype(BF16))
        krs.append(rotated(ko + h * RET_QK_DIM) * (RET_QK_DIM ** -0.5))
        vrs.append(jnp.where(v_valid, vg_ref[:, h * RET_V_DIM:(h + 1) * RET_V_DIM].astype(F32), 0.0).astype(BF16))
        prevs.append(rstate_ref[h])
    scores = [_dot_nt(qr, kr.astype(BF16)) for qr, kr in zip(qrs, krs)]
    y_crs = [_dot(qr, prev.astype(BF16)) for qr, prev in zip(qrs, prevs)]
    kvs = [_dot_tn((krs[h] * kdec_ref[h]).astype(BF16), vrs[h]) for h in range(RET_HEADS)]
    for h in range(RET_HEADS):
        log_gamma = math.log(1.0 - 2.0 ** (-5.0 - h))
        rstate_ref[h] = prevs[h] * math.exp(log_gamma * CHUNK) + kvs[h]
    y_ins = [_dot((scores[h] * dmask_ref[h]).astype(BF16), vrs[h]) for h in range(RET_HEADS)]
    yc = []
    for h in range(RET_HEADS):
        gate = vg_ref[:, go + h * RET_V_DIM:go + (h + 1) * RET_V_DIM].astype(F32)
        yh = y_ins[h] + y_crs[h] * qdec_ref[h]
        mu = jnp.mean(yh, axis=-1, keepdims=True)
        yh = yh - mu
        var = jnp.mean(yh * yh, axis=-1, keepdims=True)
        yc.append(yh * lax.rsqrt(var + EPS) * _silu(gate))
    yc_ref[...] = jnp.concatenate(yc, axis=-1).astype(yc_ref.dtype)


def _mixers(proj, dt_raw, rope, conv_a, sconv_w, sconv_b, dt_bias, a_log, d_skip, snorm, expand):
    n_rows = proj.shape[0]
    nc = n_rows // CHUNK
    chunk_cos, chunk_sin, base_cos, base_sin = rope

    def col(width, start):
        assert start % width == 0
        return pl.BlockSpec((CHUNK, width), lambda c, s=start // width: (c, s))

    def full(arr):
        return pl.BlockSpec(arr.shape, lambda c: (0,) * arr.ndim)

    row_block = pl.BlockSpec((CHUNK, D_MODEL), lambda c: (c, 0))
    chunk_row = pl.BlockSpec((1, 1, RET_QK_DIM // 2), lambda c: (c, 0, 0))
    weights = (base_cos, base_sin, conv_a, sconv_w, sconv_b, dt_bias, a_log, d_skip, snorm, expand)
    return pl.pallas_call(
        _mixer_kernel,
        grid=(nc,),
        in_specs=[
            col(3 * D_MODEL, COL_CONV), col(D_MODEL, COL_SSD_Z), col(SSD_CONV_DIM, COL_SSD_XBC),
            col(2048, COL_RET_QK), col(2048, COL_RET_VG),
            pl.BlockSpec((CHUNK, DT_LANES), lambda c: (c, 0)), chunk_row, chunk_row,
        ] + [full(w) for w in weights],
        out_specs=[row_block, row_block, row_block],
        out_shape=[jax.ShapeDtypeStruct((n_rows, D_MODEL), BF16)] * 3,
        scratch_shapes=[
            pltpu.VMEM((HIST + CHUNK, D_MODEL), F32),
            pltpu.VMEM((HIST + CHUNK, SSD_CONV_DIM), F32),
            pltpu.VMEM((CHUNK, SSD_INNER), F32),
            pltpu.VMEM((CHUNK, 2 * SSD_GROUPS * SSD_STATE), BF16),
            pltpu.VMEM((SSD_GROUPS, SSD_STATE, SSD_GROUP_WIDTH), F32),
            pltpu.VMEM((RET_HEADS, RET_QK_DIM, RET_V_DIM), F32),
            pltpu.VMEM((RET_HEADS, CHUNK, CHUNK), F32),
            pltpu.VMEM((RET_HEADS, CHUNK, RET_QK_DIM), F32),
            pltpu.VMEM((RET_HEADS, CHUNK, RET_QK_DIM), F32),
        ],
        compiler_params=pltpu.CompilerParams(
            dimension_semantics=("arbitrary",), vmem_limit_bytes=VMEM_LIMIT),
    )(proj, proj, proj, proj, proj, dt_raw, chunk_cos, chunk_sin, *weights)


def _sb_kernel(q_ref, k_ref, v_ref, proj_hbm, o_ref, kbuf, vbuf, acc_ref, run_ref, sem):
    i = pl.program_id(0)
    kbuf[i % SB_RING] = k_ref[...]
    vbuf[i % SB_RING] = v_ref[...]
    acc_ref[...] = jnp.zeros_like(acc_ref)
    run_ref[...] = jnp.zeros_like(run_ref)
    scale = SB_HEAD_DIM ** -0.5
    heads = [slice(h * SB_HEAD_DIM, (h + 1) * SB_HEAD_DIM) for h in range(SB_HEADS)]
    sfx_r = lax.broadcasted_iota(jnp.int32, (CHUNK, CHUNK), 0)
    sfx_c = lax.broadcasted_iota(jnp.int32, (CHUNK, CHUNK), 1)
    suffix = (sfx_r >= sfx_c).astype(BF16)

    def fetch(j):
        far = (i - j) >= SB_RING

        @pl.when(far)
        def _():
            rows_j = pl.ds(pl.multiple_of(j * CHUNK, CHUNK), CHUNK)
            copies = [
                pltpu.make_async_copy(proj_hbm.at[rows_j, pl.ds(COL_SB_K, D_MODEL)], kbuf.at[SB_RING], sem.at[0]),
                pltpu.make_async_copy(proj_hbm.at[rows_j, pl.ds(COL_SB_V, D_MODEL)], vbuf.at[SB_RING], sem.at[1]),
            ]
            for cp in copies:
                cp.start()
            for cp in copies:
                cp.wait()

        return jnp.where(far, SB_RING, j % SB_RING)

    def visit(j, slot, causal, pad, n):
        rows = lax.broadcasted_iota(jnp.int32, (n, CHUNK), 0)
        cols = lax.broadcasted_iota(jnp.int32, (n, CHUNK), 1)
        mask = None
        if causal:
            mask = cols < rows
        if pad:
            in_seq = (cols + j * CHUNK) >= PAD
            mask = in_seq if mask is None else mask & in_seq
        zs = [_dot_nt(q_ref[0:n, hs], kbuf[slot, :, hs]) * scale for hs in heads]
        csums = []
        for z in zs:
            sp = jnp.maximum(z, 0.0) + jnp.log(1.0 + jnp.exp(-jnp.abs(z)))
            if mask is not None:
                sp = jnp.where(mask, sp, 0.0)
            csums.append(_dot(sp.astype(BF16), suffix))
        run_min = None
        outs = []
        for h, (z, csum) in enumerate(zip(zs, csums)):
            run = run_ref[h, 0:n, :]
            w = jnp.exp(z - csum - run)
            if mask is not None:
                w = jnp.where(mask, w, 0.0)
            outs.append(_dot(w.astype(BF16), vbuf[slot, :, heads[h]]))
            run = run + csum[:, 0:1]
            run_ref[h, 0:n, :] = run
            run_min = run if run_min is None else jnp.minimum(run_min, run)
        acc_ref[0:n, :] += jnp.concatenate(outs, axis=-1)
        top = jnp.min(run_min[0:SB_TOP])
        rest = jnp.min(run_min[SB_TOP:n]) if n > SB_TOP else None
        return top, rest

    def step(j, top, rest, pad):
        slot = fetch(j)

        def top_rows():
            return visit(j, slot, False, pad, SB_TOP)[0], rest

        def all_rows():
            return visit(j, slot, False, pad, CHUNK)

        return lax.cond(rest >= SB_UNDERFLOW, top_rows, all_rows)

    top, rest = visit(i, i % SB_RING, True, True, CHUNK)

    def body(carry):
        j, top, rest = carry
        top, rest = step(j, top, rest, False)
        return j - 1, top, rest

    def cond(carry):
        j, top, rest = carry
        return (j >= 1) & (jnp.minimum(top, rest) < SB_UNDERFLOW)

    j, top, rest = lax.while_loop(cond, body, (i - 1, top, rest))

    @pl.when((j == 0) & (jnp.minimum(top, rest) < SB_UNDERFLOW))
    def _():
        step(0, top, rest, True)

    o_ref[...] = acc_ref[...].astype(o_ref.dtype)


def _stick_breaking(proj):
    n_rows = proj.shape[0]

    def col(start):
        assert start % D_MODEL == 0
        return pl.BlockSpec((CHUNK, D_MODEL), lambda i, s=start // D_MODEL: (i, s))

    return pl.pallas_call(
        _sb_kernel,
        grid=(n_rows // CHUNK,),
        in_specs=[col(COL_SB_Q), col(COL_SB_K), col(COL_SB_V), pl.BlockSpec(memory_space=pl.ANY)],
        out_specs=pl.BlockSpec((CHUNK, D_MODEL), lambda i: (i, 0)),
        out_shape=jax.ShapeDtypeStruct((n_rows, D_MODEL), BF16),
        scratch_shapes=[
            pltpu.VMEM((SB_RING + 1, CHUNK, D_MODEL), BF16),
            pltpu.VMEM((SB_RING + 1, CHUNK, D_MODEL), BF16),
            pltpu.VMEM((CHUNK, D_MODEL), F32),
            pltpu.VMEM((SB_HEADS, CHUNK, 1), F32),
            pltpu.SemaphoreType.DMA((2,)),
        ],
        compiler_params=pltpu.CompilerParams(
            dimension_semantics=("arbitrary",), vmem_limit_bytes=VMEM_LIMIT),
    )(proj, proj, proj, proj)


def _merge_kernel(ya_ref, yb_ref, yc_ref, yd_ref, ga_ref, gb_ref, gc_ref, gd_ref, h_ref,
                  wb_ref, wo_ref, nw_ref, o_ref):
    merged = None
    branches = ((ya_ref, ga_ref), (yb_ref, gb_ref), (yc_ref, gc_ref), (yd_ref, gd_ref))
    for n, (y_ref, g_ref) in enumerate(branches):
        up = _dot(y_ref[...], wb_ref[n])
        gate = _sigmoid(g_ref[...].astype(F32))
        merged = gate * up if merged is None else merged + gate * up
    mix = _dot(merged.astype(BF16), wo_ref[...])
    o_ref[...] = h_ref[...] + _rmsnorm(mix, nw_ref[...])


def _merge(ya, yb, yc, yd, proj, h, w_branch, w_out, norm_w, layer):
    n_rows = h.shape[0]
    tm = _row_tile(n_rows, 384)
    row_block = pl.BlockSpec((tm, D_MODEL), lambda i: (i, 0))
    return pl.pallas_call(
        _merge_kernel,
        grid=(n_rows // tm,),
        in_specs=[
            row_block, row_block, row_block, row_block,
        ] + [
            pl.BlockSpec((tm, D_MODEL), lambda i, s=COL_GATE // D_MODEL + n: (i, s)) for n in range(N_BRANCH)
        ] + [
            row_block,
            pl.BlockSpec((None, N_BRANCH, D_MODEL, D_MODEL), lambda i: (layer, 0, 0, 0)),
            pl.BlockSpec((None, D_MODEL, D_MODEL), lambda i: (layer, 0, 0)),
            pl.BlockSpec((1, D_MODEL), lambda i: (0, 0)),
        ],
        out_specs=row_block,
        out_shape=jax.ShapeDtypeStruct((n_rows, D_MODEL), F32),
        compiler_params=pltpu.CompilerParams(
            dimension_semantics=("parallel",), vmem_limit_bytes=VMEM_LIMIT),
    )(ya, yb, yc, yd, proj, proj, proj, proj, h, w_branch, w_out, norm_w)


def _ffn_kernel(h_ref, w1_ref, w2_ref, n1_ref, n2_ref, o_ref):
    h = h_ref[...]
    xn = _rmsnorm(h, n1_ref[...]).astype(BF16)
    f = _dot(xn, w1_ref[...])
    act = (_silu(f[:, 0:D_FF]) * f[:, D_FF:2 * D_FF]).astype(BF16)
    o_ref[...] = h + _rmsnorm(_dot(act, w2_ref[...]), n2_ref[...])


def _ffn(h, w1, w2, n1, n2, layer):
    n_rows = h.shape[0]
    tm = _row_tile(n_rows, 384)
    row_block = pl.BlockSpec((tm, D_MODEL), lambda i: (i, 0))
    return pl.pallas_call(
        _ffn_kernel,
        grid=(n_rows // tm,),
        in_specs=[
            row_block,
            pl.BlockSpec((None, D_MODEL, 2 * D_FF), lambda i: (layer, 0, 0)),
            pl.BlockSpec((None, D_FF, D_MODEL), lambda i: (layer, 0, 0)),
            pl.BlockSpec((1, D_MODEL), lambda i: (0, 0)),
            pl.BlockSpec((1, D_MODEL), lambda i: (0, 0)),
        ],
        out_specs=row_block,
        out_shape=jax.ShapeDtypeStruct((n_rows, D_MODEL), F32),
        compiler_params=pltpu.CompilerParams(
            dimension_semantics=("parallel",), vmem_limit_bytes=VMEM_LIMIT),
    )(h, w1, w2, n1, n2)


def _rope_tables(n_chunks):
    half = RET_QK_DIM // 2
    inv = ROPE_BASE ** (-np.arange(half, dtype=np.float64) / half)
    chunk_ang = (np.arange(n_chunks, dtype=np.float64) * CHUNK)[:, None, None] * inv[None, None, :]
    base_ang = np.arange(CHUNK, dtype=np.float64)[:, None] * inv[None, :]
    return tuple(jnp.asarray(t, F32) for t in
                 (np.cos(chunk_ang), np.sin(chunk_ang), np.cos(base_ang), np.sin(base_ang)))


def _head_expand_matrix():
    e = np.zeros((DT_LANES, SSD_INNER), np.float32)
    for h in range(SSD_HEADS):
        e[h, h * SSD_HEAD_DIM:(h + 1) * SSD_HEAD_DIM] = 1.0
    return jnp.asarray(e, BF16)


def _pad_lanes(v):
    return jnp.pad(v.astype(F32), (0, DT_LANES - v.shape[0]))[None, :]


def kernel(x, meta, w_in, conv_a, ssd_conv_w, ssd_conv_b, ssd_dt_bias, ssd_a_log, ssd_d, ssd_norm,
           w_branch, w_out, w_ffn_in, w_ffn_out, norm_mix_pre, norm_mix_post, norm_ffn_pre,
           norm_ffn_post):
    batch, seq, _ = x.shape
    assert batch == 1 and seq % CHUNK == 0
    depth = w_in.shape[0]
    n_rows = seq + CHUNK
    h = jnp.concatenate([jnp.zeros((PAD, D_MODEL), F32), meta.astype(F32), x[0].astype(F32)], axis=0)
    rope = _rope_tables(n_rows // CHUNK)
    expand = _head_expand_matrix()
    dt_end = DT_COL_START + SSD_HEADS
    w_main = jnp.concatenate([w_in[:, :, :DT_COL_START], w_in[:, :, dt_end:]], axis=2).astype(BF16)
    w_dt = jnp.pad(w_in[:, :, DT_COL_START:dt_end], ((0, 0), (0, 0), (0, DT_LANES - SSD_HEADS))).astype(BF16)
    w_branch_b, w_out_b = w_branch.astype(BF16), w_out.astype(BF16)
    w_ffn_in_b, w_ffn_out_b = w_ffn_in.astype(BF16), w_ffn_out.astype(BF16)
    for l in range(depth):
        proj, dt_raw = _inproj(h, norm_mix_pre[l][None, :], w_main, w_dt, l)
        ya, yb, yc = _mixers(
            proj, dt_raw, rope, conv_a[l], ssd_conv_w[l], ssd_conv_b[l][None, :],
            _pad_lanes(ssd_dt_bias[l]), _pad_lanes(ssd_a_log[l]),
            jnp.repeat(ssd_d[l].astype(F32), SSD_HEAD_DIM)[None, :], ssd_norm[l][None, :], expand)
        yd = _stick_breaking(proj)
        h = _merge(ya, yb, yc, yd, proj, h, w_branch_b, w_out_b, norm_mix_post[l][None, :], l)
        h = _ffn(h, w_ffn_in_b, w_ffn_out_b, norm_ffn_pre[l][None, :], norm_ffn_post[l][None, :], l)
    return h[CHUNK:][None].astype(x.dtype)
```

```python
import functools
import math

import numpy as np
import jax
import jax.numpy as jnp
from jax import lax
from jax.experimental import pallas as pl
from jax.experimental.pallas import tpu as pltpu

F32 = jnp.float32
BF16 = jnp.bfloat16

D_MODEL = 1024
N_META = 16
CHUNK = 128
PAD = CHUNK - N_META
EPS = 1e-6

SSD_HEAD_DIM = 64
SSD_HEADS = 16
SSD_INNER = 1024
SSD_GROUPS = 4
SSD_STATE = 128
SSD_CONV_K = 4
SSD_CONV_DIM = SSD_INNER + 2 * SSD_GROUPS * SSD_STATE
SSD_GROUP_WIDTH = SSD_INNER // SSD_GROUPS
CONV_A_K = 3
RET_HEADS = 4
RET_QK_DIM = 256
RET_V_DIM = 256
ROPE_BASE = 10000.0
SB_HEADS = 8
SB_HEAD_DIM = 128
N_BRANCH = 4
D_FF = 2816

COL_CONV = 0
COL_SSD_Z = 3072
COL_SSD_XBC = 4096
COL_RET_QK = 6144
COL_RET_VG = 8192
COL_SB_Q = 10240
COL_SB_K = 11264
COL_SB_V = 12288
COL_GATE = 13312
PROJ_WIDTH = 17408
DT_COL_START = 6144
DT_LANES = 128
CONV_STRIP = 256
FFN_FINAL_TILE = 512

HIST = 8
SB_UNDERFLOW = 104.0
SB_RING = 4
SB_TOP = 32
VMEM_LIMIT = 56 * 1024 * 1024


def _rmsnorm(x, w):
    return x * lax.rsqrt(jnp.mean(x * x, axis=-1, keepdims=True) + EPS) * w


def _sigmoid(x):
    return 1.0 / (1.0 + jnp.exp(-x))


def _silu(x):
    return x * _sigmoid(x)


def _softplus(x):
    return jnp.maximum(x, 0.0) + jnp.log1p(jnp.exp(-jnp.abs(x)))


def _split3(x):
    hi = x.astype(BF16)
    r = x - hi.astype(F32)
    mid = r.astype(BF16)
    lo = (r - mid.astype(F32)).astype(BF16)
    return hi, mid, lo


def _dot(a, b):
    return jnp.dot(a, b, preferred_element_type=F32)


def _dot_nt(a, b):
    return lax.dot_general(a, b, (((1,), (1,)), ((), ())), preferred_element_type=F32)


def _dot_tn(a, b):
    return lax.dot_general(a, b, (((0,), (0,)), ((), ())), preferred_element_type=F32)


def _row_tile(n_rows, target):
    best = 16
    for t in range(16, min(n_rows, target) + 1, 16):
        if n_rows % t == 0:
            best = t
    return best


def _inproj_kernel(x_ref, nw_ref, w_ref, wdt_ref, o_ref, dt_ref, xn_ref):
    @pl.when(pl.program_id(1) == 0)
    def _():
        xn = _rmsnorm(x_ref[...], nw_ref[...]).astype(BF16)
        xn_ref[...] = xn
        dt_ref[...] = _dot(xn, wdt_ref[...])

    o_ref[...] = _dot(xn_ref[...], w_ref[...]).astype(o_ref.dtype)


def _inproj(h, norm_w, w_main, w_dt, layer):
    n_rows = h.shape[0]
    tm = _row_tile(n_rows, 1376)
    tn = 1024
    return pl.pallas_call(
        _inproj_kernel,
        grid=(n_rows // tm, PROJ_WIDTH // tn),
        in_specs=[
            pl.BlockSpec((tm, D_MODEL), lambda i, j: (i, 0)),
            pl.BlockSpec((1, D_MODEL), lambda i, j: (0, 0)),
            pl.BlockSpec((None, D_MODEL, tn), lambda i, j: (layer, 0, j)),
            pl.BlockSpec((None, D_MODEL, DT_LANES), lambda i, j: (layer, 0, 0)),
        ],
        out_specs=[
            pl.BlockSpec((tm, tn), lambda i, j: (i, j)),
            pl.BlockSpec((tm, DT_LANES), lambda i, j: (i, 0)),
        ],
        out_shape=[
            jax.ShapeDtypeStruct((n_rows, PROJ_WIDTH), BF16),
            jax.ShapeDtypeStruct((n_rows, DT_LANES), F32),
        ],
        scratch_shapes=[pltpu.VMEM((tm, D_MODEL), BF16)],
        compiler_params=pltpu.CompilerParams(
            dimension_semantics=("parallel", "arbitrary"), vmem_limit_bytes=VMEM_LIMIT),
    )(h, norm_w, w_main, w_dt)


def _mixer_kernel(conv_ref, z_ref, xbc_ref, qk_ref, vg_ref, dt_ref, ccos_ref, csin_ref,
                  bcos_ref, bsin_ref,
                  conva_ref, sconvw_ref, sconvb_ref, dtb_ref, alog_ref, dskip_ref, snorm_ref,
                  expand_ref,
                  ya_ref, yb_ref, yc_ref,
                  uhist_ref, xhist_ref, xs_ref, bc_ref, sstate_ref, rstate_ref, dmask_ref, kdec_ref,
                  qdec_ref):
    c = pl.program_id(0)
    li = lax.broadcasted_iota(jnp.int32, (CHUNK, CHUNK), 0)
    si = lax.broadcasted_iota(jnp.int32, (CHUNK, CHUNK), 1)
    causal = li >= si

    @pl.when(c == 0)
    def _():
        uhist_ref[0:HIST, :] = jnp.zeros((HIST, D_MODEL), F32)
        xhist_ref[0:HIST, :] = jnp.zeros((HIST, SSD_CONV_DIM), F32)
        sstate_ref[...] = jnp.zeros_like(sstate_ref)
        rstate_ref[...] = jnp.zeros_like(rstate_ref)
        rel = (li - si).astype(F32)
        row_w = lax.broadcasted_iota(jnp.int32, (CHUNK, RET_QK_DIM), 0).astype(F32)
        for h in range(RET_HEADS):
            log_gamma = math.log(1.0 - 2.0 ** (-5.0 - h))
            dmask_ref[h] = jnp.where(causal, jnp.exp(log_gamma * jnp.maximum(rel, 0.0)), 0.0)
            kdec_ref[h] = jnp.exp(log_gamma * (CHUNK - 1.0 - row_w))
            qdec_ref[h] = jnp.exp(log_gamma * (row_w + 1.0))

    def valid(shape):
        return (lax.broadcasted_iota(jnp.int32, shape, 0) + c * CHUNK) >= PAD

    strip_valid = valid((CHUNK, CONV_STRIP))

    def causal_conv(u, hist_ref, cs, taps_ref):
        k_taps = taps_ref.shape[0]
        hist_ref[HIST:HIST + CHUNK, cs] = u
        out = u * taps_ref[k_taps - 1:k_taps, cs]
        for t in range(k_taps - 1):
            off = HIST - (k_taps - 1) + t
            out = out + hist_ref[off:off + CHUNK, cs] * taps_ref[t:t + 1, cs]
        hist_ref[0:HIST, cs] = hist_ref[CHUNK:CHUNK + HIST, cs]
        return out

    for n in range(D_MODEL // CONV_STRIP):
        cs = slice(n * CONV_STRIP, (n + 1) * CONV_STRIP)
        b_gate = conv_ref[:, cs].astype(F32)
        c_gate = conv_ref[:, D_MODEL + cs.start:D_MODEL + cs.stop].astype(F32)
        xa = conv_ref[:, 2 * D_MODEL + cs.start:2 * D_MODEL + cs.stop].astype(F32)
        u = jnp.where(strip_valid, c_gate * xa, 0.0)
        ya_ref[:, cs] = (b_gate * causal_conv(u, uhist_ref, cs, conva_ref)).astype(ya_ref.dtype)

    for n in range(SSD_CONV_DIM // CONV_STRIP):
        cs = slice(n * CONV_STRIP, (n + 1) * CONV_STRIP)
        xin = jnp.where(strip_valid, xbc_ref[:, cs].astype(F32), 0.0)
        xc = _silu(causal_conv(xin, xhist_ref, cs, sconvw_ref) + sconvb_ref[:, cs])
        if cs.stop <= SSD_INNER:
            xs_ref[:, cs] = jnp.where(strip_valid, xc, 0.0)
        else:
            bc_ref[:, cs.start - SSD_INNER:cs.stop - SSD_INNER] = xc.astype(BF16)
    xs = xs_ref[...]
    bm = bc_ref[:, 0:SSD_GROUPS * SSD_STATE]
    cm = bc_ref[:, SSD_GROUPS * SSD_STATE:2 * SSD_GROUPS * SSD_STATE]

    dtv = _softplus(dt_ref[...] + dtb_ref[...])
    a = -jnp.exp(alog_ref[...]) * dtv
    tri = causal.astype(BF16)
    acs = sum(_dot(tri, part) for part in _split3(a))
    acs_t = acs.T
    acs_last = acs[CHUNK - 1:CHUNK, :]
    expand = expand_ref[...]

    def expand_heads(v):
        return sum(_dot(part, expand) for part in _split3(v))

    dt_x = expand_heads(dtv)
    eacs_x = jnp.exp(expand_heads(acs))
    dstate_x = jnp.exp(expand_heads(acs_last - acs))
    cdecay_x = eacs_x[CHUNK - 1:CHUNK, :]
    xdt = xs * dt_x
    xdt_b = xdt.astype(BF16)
    xw_b = (xdt * dstate_x).astype(BF16)

    hpg = SSD_HEADS // SSD_GROUPS
    groups = [slice(g * SSD_GROUP_WIDTH, (g + 1) * SSD_GROUP_WIDTH) for g in range(SSD_GROUPS)]
    ccs = [cm[:, g * SSD_STATE:(g + 1) * SSD_STATE] for g in range(SSD_GROUPS)]
    bcs = [bm[:, g * SSD_STATE:(g + 1) * SSD_STATE] for g in range(SSD_GROUPS)]
    cbs = [_dot_nt(cc, bc) for cc, bc in zip(ccs, bcs)]
    prevs = [sstate_ref[g] for g in range(SSD_GROUPS)]
    y_offs = [_dot(cc, prev.astype(BF16)) for cc, prev in zip(ccs, prevs)]
    new_states = [_dot_tn(bc, xw_b[:, gs]) for bc, gs in zip(bcs, groups)]
    for g in range(SSD_GROUPS):
        sstate_ref[g] = prevs[g] * cdecay_x[:, groups[g]] + new_states[g]
    yd = []
    for h in range(SSD_HEADS):
        diff = acs[:, h:h + 1] - acs_t[h:h + 1, :]
        seg = jnp.exp(jnp.where(causal, diff, -jnp.inf))
        hs = slice(h * SSD_HEAD_DIM, (h + 1) * SSD_HEAD_DIM)
        yd.append(_dot((cbs[h // hpg] * seg).astype(BF16), xdt_b[:, hs]))
    y = jnp.concatenate(yd, axis=-1) + jnp.concatenate(y_offs, axis=-1) * eacs_x + xs * dskip_ref[...]
    y = y * _silu(z_ref[...].astype(F32))
    yn = []
    for g in range(SSD_GROUPS):
        yg = y[:, g * SSD_GROUP_WIDTH:(g + 1) * SSD_GROUP_WIDTH]
        yn.append(yg * lax.rsqrt(jnp.mean(yg * yg, axis=-1, keepdims=True) + EPS))
    yb_ref[...] = (jnp.concatenate(yn, axis=-1) * snorm_ref[...]).astype(yb_ref.dtype)

    cc_, sc_ = ccos_ref[0], csin_ref[0]
    bcos, bsin = bcos_ref[...], bsin_ref[...]
    cos = cc_ * bcos - sc_ * bsin
    sin = sc_ * bcos + cc_ * bsin
    half = RET_QK_DIM // 2
    ko = RET_HEADS * RET_QK_DIM
    go = RET_HEADS * RET_V_DIM
    v_valid = valid((CHUNK, RET_V_DIM))

    def rotated(off):
        x1 = qk_ref[:, off:off + half].astype(F32)
        x2 = qk_ref[:, off + half:off + RET_QK_DIM].astype(F32)
        return jnp.concatenate([x1 * cos - x2 * sin, x1 * sin + x2 * cos], axis=-1)

    qrs, krs, vrs, prevs = [], [], [], []
    for h in range(RET_HEADS):
        qrs.append(rotated(h * RET_QK_DIM).astype(BF16))
        krs.append(rotated(ko + h * RET_QK_DIM) * (RET_QK_DIM ** -0.5))
        vrs.append(jnp.where(v_valid, vg_ref[:, h * RET_V_DIM:(h + 1) * RET_V_DIM].astype(F32), 0.0).astype(BF16))
        prevs.append(rstate_ref[h])
    scores = [_dot_nt(qr, kr.astype(BF16)) for qr, kr in zip(qrs, krs)]
    y_crs = [_dot(qr, prev.astype(BF16)) for qr, prev in zip(qrs, prevs)]
    kvs = [_dot_tn((krs[h] * kdec_ref[h]).astype(BF16), vrs[h]) for h in range(RET_HEADS)]
    for h in range(RET_HEADS):
        log_gamma = math.log(1.0 - 2.0 ** (-5.0 - h))
        rstate_ref[h] = prevs[h] * math.exp(log_gamma * CHUNK) + kvs[h]
    y_ins = [_dot((scores[h] * dmask_ref[h]).astype(BF16), vrs[h]) for h in range(RET_HEADS)]
    yc = []
    for h in range(RET_HEADS):
        gate = vg_ref[:, go + h * RET_V_DIM:go + (h + 1) * RET_V_DIM].astype(F32)
        yh = y_ins[h] + y_crs[h] * qdec_ref[h]
        mu = jnp.mean(yh, axis=-1, keepdims=True)
        yh = yh - mu
        var = jnp.mean(yh * yh, axis=-1, keepdims=True)
        yc.append(yh * lax.rsqrt(var + EPS) * _silu(gate))
    yc_ref[...] = jnp.concatenate(yc, axis=-1).astype(yc_ref.dtype)


def _mixers(proj, dt_raw, rope, conv_a, sconv_w, sconv_b, dt_bias, a_log, d_skip, snorm, expand):
    n_rows = proj.shape[0]
    nc = n_rows // CHUNK
    chunk_cos, chunk_sin, base_cos, base_sin = rope

    def col(width, start):
        assert start % width == 0
        return pl.BlockSpec((CHUNK, width), lambda c, s=start // width: (c, s))

    def full(arr):
        return pl.BlockSpec(arr.shape, lambda c: (0,) * arr.ndim)

    row_block = pl.BlockSpec((CHUNK, D_MODEL), lambda c: (c, 0))
    chunk_row = pl.BlockSpec((1, 1, RET_QK_DIM // 2), lambda c: (c, 0, 0))
    weights = (base_cos, base_sin, conv_a, sconv_w, sconv_b, dt_bias, a_log, d_skip, snorm, expand)
    return pl.pallas_call(
        _mixer_kernel,
        grid=(nc,),
        in_specs=[
            col(3 * D_MODEL, COL_CONV), col(D_MODEL, COL_SSD_Z), col(SSD_CONV_DIM, COL_SSD_XBC),
            col(2048, COL_RET_QK), col(2048, COL_RET_VG),
            pl.BlockSpec((CHUNK, DT_LANES), lambda c: (c, 0)), chunk_row, chunk_row,
        ] + [full(w) for w in weights],
        out_specs=[row_block, row_block, row_block],
        out_shape=[jax.ShapeDtypeStruct((n_rows, D_MODEL), BF16)] * 3,
        scratch_shapes=[
            pltpu.VMEM((HIST + CHUNK, D_MODEL), F32),
            pltpu.VMEM((HIST + CHUNK, SSD_CONV_DIM), F32),
            pltpu.VMEM((CHUNK, SSD_INNER), F32),
            pltpu.VMEM((CHUNK, 2 * SSD_GROUPS * SSD_STATE), BF16),
            pltpu.VMEM((SSD_GROUPS, SSD_STATE, SSD_GROUP_WIDTH), F32),
            pltpu.VMEM((RET_HEADS, RET_QK_DIM, RET_V_DIM), F32),
            pltpu.VMEM((RET_HEADS, CHUNK, CHUNK), F32),
            pltpu.VMEM((RET_HEADS, CHUNK, RET_QK_DIM), F32),
            pltpu.VMEM((RET_HEADS, CHUNK, RET_QK_DIM), F32),
        ],
        compiler_params=pltpu.CompilerParams(
            dimension_semantics=("arbitrary",), vmem_limit_bytes=VMEM_LIMIT),
    )(proj, proj, proj, proj, proj, dt_raw, chunk_cos, chunk_sin, *weights)


def _sb_kernel(q_ref, k_ref, v_ref, proj_hbm, o_ref, kbuf, vbuf, acc_ref, run_ref, sem):
    i = pl.program_id(0)
    kbuf[i % SB_RING] = k_ref[...]
    vbuf[i % SB_RING] = v_ref[...]
    acc_ref[...] = jnp.zeros_like(acc_ref)
    run_ref[...] = jnp.zeros_like(run_ref)
    scale = SB_HEAD_DIM ** -0.5
    heads = [slice(h * SB_HEAD_DIM, (h + 1) * SB_HEAD_DIM) for h in range(SB_HEADS)]
    sfx_r = lax.broadcasted_iota(jnp.int32, (CHUNK, CHUNK), 0)
    sfx_c = lax.broadcasted_iota(jnp.int32, (CHUNK, CHUNK), 1)
    suffix = (sfx_r >= sfx_c).astype(BF16)

    def fetch(j):
        far = (i - j) >= SB_RING

        @pl.when(far)
        def _():
            rows_j = pl.ds(pl.multiple_of(j * CHUNK, CHUNK), CHUNK)
            copies = [
                pltpu.make_async_copy(proj_hbm.at[rows_j, pl.ds(COL_SB_K, D_MODEL)], kbuf.at[SB_RING], sem.at[0]),
                pltpu.make_async_copy(proj_hbm.at[rows_j, pl.ds(COL_SB_V, D_MODEL)], vbuf.at[SB_RING], sem.at[1]),
            ]
            for cp in copies:
                cp.start()
            for cp in copies:
                cp.wait()

        return jnp.where(far, SB_RING, j % SB_RING)

    def visit(j, slot, causal, pad, n):
        rows = lax.broadcasted_iota(jnp.int32, (n, CHUNK), 0)
        cols = lax.broadcasted_iota(jnp.int32, (n, CHUNK), 1)
        mask = None
        if causal:
            mask = cols < rows
        if pad:
            in_seq = (cols + j * CHUNK) >= PAD
            mask = in_seq if mask is None else mask & in_seq
        zs = [_dot_nt(q_ref[0:n, hs], kbuf[slot, :, hs]) * scale for hs in heads]
        csums = []
        for z in zs:
            sp = jnp.maximum(z, 0.0) + jnp.log(1.0 + jnp.exp(-jnp.abs(z)))
            if mask is not None:
                sp = jnp.where(mask, sp, 0.0)
            csums.append(_dot(sp.astype(BF16), suffix))
        run_min = None
        outs = []
        for h, (z, csum) in enumerate(zip(zs, csums)):
            run = run_ref[h, 0:n, :]
            w = jnp.exp(z - csum - run)
            if mask is not None:
                w = jnp.where(mask, w, 0.0)
            outs.append(_dot(w.astype(BF16), vbuf[slot, :, heads[h]]))
            run = run + csum[:, 0:1]
            run_ref[h, 0:n, :] = run
            run_min = run if run_min is None else jnp.minimum(run_min, run)
        acc_ref[0:n, :] += jnp.concatenate(outs, axis=-1)
        top = jnp.min(run_min[0:SB_TOP])
        rest = jnp.min(run_min[SB_TOP:n]) if n > SB_TOP else None
        return top, rest

    def step(j, top, rest, pad):
        slot = fetch(j)

        def top_rows():
            return visit(j, slot, False, pad, SB_TOP)[0], rest

        def all_rows():
            return visit(j, slot, False, pad, CHUNK)

        return lax.cond(rest >= SB_UNDERFLOW, top_rows, all_rows)

    top, rest = visit(i, i % SB_RING, True, True, CHUNK)

    def body(carry):
        j, top, rest = carry
        top, rest = step(j, top, rest, False)
        return j - 1, top, rest

    def cond(carry):
        j, top, rest = carry
        return (j >= 1) & (jnp.minimum(top, rest) < SB_UNDERFLOW)

    j, top, rest = lax.while_loop(cond, body, (i - 1, top, rest))

    @pl.when((j == 0) & (jnp.minimum(top, rest) < SB_UNDERFLOW))
    def _():
        step(0, top, rest, True)

    o_ref[...] = acc_ref[...].astype(o_ref.dtype)


def _stick_breaking(proj):
    n_rows = proj.shape[0]

    def col(start):
        assert start % D_MODEL == 0
        return pl.BlockSpec((CHUNK, D_MODEL), lambda i, s=start // D_MODEL: (i, s))

    return pl.pallas_call(
        _sb_kernel,
        grid=(n_rows // CHUNK,),
        in_specs=[col(COL_SB_Q), col(COL_SB_K), col(COL_SB_V), pl.BlockSpec(memory_space=pl.ANY)],
        out_specs=pl.BlockSpec((CHUNK, D_MODEL), lambda i: (i, 0)),
        out_shape=jax.ShapeDtypeStruct((n_rows, D_MODEL), BF16),
        scratch_shapes=[
            pltpu.VMEM((SB_RING + 1, CHUNK, D_MODEL), BF16),
            pltpu.VMEM((SB_RING + 1, CHUNK, D_MODEL), BF16),
            pltpu.VMEM((CHUNK, D_MODEL), F32),
            pltpu.VMEM((SB_HEADS, CHUNK, 1), F32),
            pltpu.SemaphoreType.DMA((2,)),
        ],
        compiler_params=pltpu.CompilerParams(
            dimension_semantics=("arbitrary",), vmem_limit_bytes=VMEM_LIMIT),
    )(proj, proj, proj, proj)


def _merge_kernel(ya_ref, yb_ref, yc_ref, yd_ref, ga_ref, gb_ref, gc_ref, gd_ref, h_ref,
                  wb_ref, wo_ref, nw_ref, o_ref):
    merged = None
    branches = ((ya_ref, ga_ref), (yb_ref, gb_ref), (yc_ref, gc_ref), (yd_ref, gd_ref))
    for n, (y_ref, g_ref) in enumerate(branches):
        up = _dot(y_ref[...], wb_ref[n])
        gate = _sigmoid(g_ref[...].astype(F32))
        merged = gate * up if merged is None else merged + gate * up
    mix = _dot(merged.astype(BF16), wo_ref[...])
    o_ref[...] = h_ref[...] + _rmsnorm(mix, nw_ref[...])


def _merge(ya, yb, yc, yd, proj, h, w_branch, w_out, norm_w, layer):
    n_rows = h.shape[0]
    tm = _row_tile(n_rows, 384)
    row_block = pl.BlockSpec((tm, D_MODEL), lambda i: (i, 0))
    return pl.pallas_call(
        _merge_kernel,
        grid=(n_rows // tm,),
        in_specs=[
            row_block, row_block, row_block, row_block,
        ] + [
            pl.BlockSpec((tm, D_MODEL), lambda i, s=COL_GATE // D_MODEL + n: (i, s)) for n in range(N_BRANCH)
        ] + [
            row_block,
            pl.BlockSpec((None, N_BRANCH, D_MODEL, D_MODEL), lambda i: (layer, 0, 0, 0),
                         pipeline_mode=pl.Buffered(1)),
            pl.BlockSpec((None, D_MODEL, D_MODEL), lambda i: (layer, 0, 0), pipeline_mode=pl.Buffered(1)),
            pl.BlockSpec((1, D_MODEL), lambda i: (0, 0)),
        ],
        out_specs=row_block,
        out_shape=jax.ShapeDtypeStruct((n_rows, D_MODEL), F32),
        compiler_params=pltpu.CompilerParams(
            dimension_semantics=("parallel",), vmem_limit_bytes=VMEM_LIMIT),
    )(ya, yb, yc, yd, proj, proj, proj, proj, h, w_branch, w_out, norm_w)


def _ffn_kernel(*refs, n_pieces):
    h_refs = refs[:n_pieces]
    w1_ref, w2_ref, n1_ref, n2_ref, o_ref = refs[n_pieces:]
    h = h_refs[0][...] if n_pieces == 1 else jnp.concatenate([r[...] for r in h_refs], axis=0)
    xn = _rmsnorm(h, n1_ref[...]).astype(BF16)
    f = _dot(xn, w1_ref[...])
    act = (_silu(f[:, 0:D_FF]) * f[:, D_FF:2 * D_FF]).astype(BF16)
    o_ref[...] = h + _rmsnorm(_dot(act, w2_ref[...]), n2_ref[...])


def _ffn(h, w1, w2, n1, n2, layer, drop_first_chunk):
    n_rows = h.shape[0]
    if drop_first_chunk:
        n_out = n_rows - CHUNK
        tm = max(t for t in range(CHUNK, FFN_FINAL_TILE + 1, CHUNK) if n_out % t == 0)
        n_pieces = tm // CHUNK
        h_specs = [pl.BlockSpec((CHUNK, D_MODEL), lambda i, k=k: (i * n_pieces + 1 + k, 0))
                   for k in range(n_pieces)]
    else:
        n_out = n_rows
        tm = _row_tile(n_rows, 384)
        n_pieces = 1
        h_specs = [pl.BlockSpec((tm, D_MODEL), lambda i: (i, 0))]
    resident = pl.Buffered(1)
    return pl.pallas_call(
        functools.partial(_ffn_kernel, n_pieces=n_pieces),
        grid=(n_out // tm,),
        in_specs=h_specs + [
            pl.BlockSpec((None, D_MODEL, 2 * D_FF), lambda i: (layer, 0, 0), pipeline_mode=resident),
            pl.BlockSpec((None, D_FF, D_MODEL), lambda i: (layer, 0, 0), pipeline_mode=resident),
            pl.BlockSpec((1, D_MODEL), lambda i: (0, 0)),
            pl.BlockSpec((1, D_MODEL), lambda i: (0, 0)),
        ],
        out_specs=pl.BlockSpec((tm, D_MODEL), lambda i: (i, 0)),
        out_shape=jax.ShapeDtypeStruct((n_out, D_MODEL), F32),
        compiler_params=pltpu.CompilerParams(
            dimension_semantics=("parallel",), vmem_limit_bytes=VMEM_LIMIT),
    )(*([h] * n_pieces), w1, w2, n1, n2)


def _rope_tables(n_chunks):
    half = RET_QK_DIM // 2
    inv = ROPE_BASE ** (-np.arange(half, dtype=np.float64) / half)
    chunk_ang = (np.arange(n_chunks, dtype=np.float64) * CHUNK)[:, None, None] * inv[None, None, :]
    base_ang = np.arange(CHUNK, dtype=np.float64)[:, None] * inv[None, :]
    return tuple(jnp.asarray(t, F32) for t in
                 (np.cos(chunk_ang), np.sin(chunk_ang), np.cos(base_ang), np.sin(base_ang)))


def _head_expand_matrix():
    e = np.zeros((DT_LANES, SSD_INNER), np.float32)
    for h in range(SSD_HEADS):
        e[h, h * SSD_HEAD_DIM:(h + 1) * SSD_HEAD_DIM] = 1.0
    return jnp.asarray(e, BF16)


def _pad_lanes(v):
    return jnp.pad(v.astype(F32), (0, DT_LANES - v.shape[0]))[None, :]


def kernel(x, meta, w_in, conv_a, ssd_conv_w, ssd_conv_b, ssd_dt_bias, ssd_a_log, ssd_d, ssd_norm,
           w_branch, w_out, w_ffn_in, w_ffn_out, norm_mix_pre, norm_mix_post, norm_ffn_pre,
           norm_ffn_post):
    batch, seq, _ = x.shape
    assert batch == 1 and seq % CHUNK == 0
    depth = w_in.shape[0]
    n_rows = seq + CHUNK
    h = jnp.concatenate([jnp.zeros((PAD, D_MODEL), F32), meta.astype(F32), x[0].astype(F32)], axis=0)
    rope = _rope_tables(n_rows // CHUNK)
    expand = _head_expand_matrix()
    dt_end = DT_COL_START + SSD_HEADS
    w_main = jnp.concatenate([w_in[:, :, :DT_COL_START], w_in[:, :, dt_end:]], axis=2).astype(BF16)
    w_dt = jnp.pad(w_in[:, :, DT_COL_START:dt_end], ((0, 0), (0, 0), (0, DT_LANES - SSD_HEADS))).astype(BF16)
    w_branch_b, w_out_b = w_branch.astype(BF16), w_out.astype(BF16)
    w_ffn_in_b, w_ffn_out_b = w_ffn_in.astype(BF16), w_ffn_out.astype(BF16)
    for l in range(depth):
        proj, dt_raw = _inproj(h, norm_mix_pre[l][None, :], w_main, w_dt, l)
        ya, yb, yc = _mixers(
            proj, dt_raw, rope, conv_a[l], ssd_conv_w[l], ssd_conv_b[l][None, :],
            _pad_lanes(ssd_dt_bias[l]), _pad_lanes(ssd_a_log[l]),
            jnp.repeat(ssd_d[l].astype(F32), SSD_HEAD_DIM)[None, :], ssd_norm[l][None, :], expand)
        yd = _stick_breaking(proj)
        h = _merge(ya, yb, yc, yd, proj, h, w_branch_b, w_out_b, norm_mix_post[l][None, :], l)
        h = _ffn(h, w_ffn_in_b, w_ffn_out_b, norm_ffn_pre[l][None, :], norm_ffn_post[l][None, :], l,
                 drop_first_chunk=(l == depth - 1))
    return h[None].astype(x.dtype)
```

```python
import functools
import math

import numpy as np
import jax
import jax.numpy as jnp
from jax import lax
from jax.experimental import pallas as pl
from jax.experimental.pallas import tpu as pltpu

F32 = jnp.float32
BF16 = jnp.bfloat16

D_MODEL = 1024
N_META = 16
CHUNK = 128
PAD = CHUNK - N_META
EPS = 1e-6

SSD_HEAD_DIM = 64
SSD_HEADS = 16
SSD_INNER = 1024
SSD_GROUPS = 4
SSD_STATE = 128
SSD_CONV_K = 4
SSD_CONV_DIM = SSD_INNER + 2 * SSD_GROUPS * SSD_STATE
SSD_GROUP_WIDTH = SSD_INNER // SSD_GROUPS
CONV_A_K = 3
RET_HEADS = 4
RET_QK_DIM = 256
RET_V_DIM = 256
ROPE_BASE = 10000.0
SB_HEADS = 8
SB_HEAD_DIM = 128
N_BRANCH = 4
D_FF = 2816

COL_CONV = 0
COL_SSD_Z = 3072
COL_SSD_XBC = 4096
COL_RET_QK = 6144
COL_RET_VG = 8192
COL_SB_Q = 10240
COL_SB_K = 11264
COL_SB_V = 12288
COL_GATE = 13312
PROJ_WIDTH = 17408
DT_COL_START = 6144
DT_LANES = 128
CONV_STRIP = 256
FFN_FINAL_TILE = 512

HIST = 8
SB_UNDERFLOW = 104.0
SB_RING = 4
SB_TOP = 32
VMEM_LIMIT = 56 * 1024 * 1024


def _rmsnorm(x, w):
    return x * lax.rsqrt(jnp.mean(x * x, axis=-1, keepdims=True) + EPS) * w


def _sigmoid(x):
    return 1.0 / (1.0 + jnp.exp(-x))


def _silu(x):
    return x * _sigmoid(x)


def _softplus(x):
    return jnp.maximum(x, 0.0) + jnp.log1p(jnp.exp(-jnp.abs(x)))


def _split3(x):
    hi = x.astype(BF16)
    r = x - hi.astype(F32)
    mid = r.astype(BF16)
    lo = (r - mid.astype(F32)).astype(BF16)
    return hi, mid, lo


def _dot(a, b):
    return jnp.dot(a, b, preferred_element_type=F32)


def _dot_nt(a, b):
    return lax.dot_general(a, b, (((1,), (1,)), ((), ())), preferred_element_type=F32)


def _dot_tn(a, b):
    return lax.dot_general(a, b, (((0,), (0,)), ((), ())), preferred_element_type=F32)


def _row_tile(n_rows, target):
    best = 16
    for t in range(16, min(n_rows, target) + 1, 16):
        if n_rows % t == 0:
            best = t
    return best


def _inproj_kernel(x_ref, nw_ref, w_ref, wdt_ref, o_ref, dt_ref, xn_ref):
    @pl.when(pl.program_id(1) == 0)
    def _():
        xn = _rmsnorm(x_ref[...], nw_ref[...]).astype(BF16)
        xn_ref[...] = xn
        dt_ref[...] = _dot_nt(xn, wdt_ref[...])

    o_ref[...] = _dot_nt(xn_ref[...], w_ref[...]).astype(o_ref.dtype)


def _inproj(h, norm_w, w_main, w_dt, layer):
    n_rows = h.shape[0]
    tm = _row_tile(n_rows, 1376)
    tn = 1024
    return pl.pallas_call(
        _inproj_kernel,
        grid=(n_rows // tm, PROJ_WIDTH // tn),
        in_specs=[
            pl.BlockSpec((tm, D_MODEL), lambda i, j: (i, 0)),
            pl.BlockSpec((1, D_MODEL), lambda i, j: (0, 0)),
            pl.BlockSpec((None, tn, D_MODEL), lambda i, j: (layer, j, 0)),
            pl.BlockSpec((None, DT_LANES, D_MODEL), lambda i, j: (layer, 0, 0)),
        ],
        out_specs=[
            pl.BlockSpec((tm, tn), lambda i, j: (i, j)),
            pl.BlockSpec((tm, DT_LANES), lambda i, j: (i, 0)),
        ],
        out_shape=[
            jax.ShapeDtypeStruct((n_rows, PROJ_WIDTH), BF16),
            jax.ShapeDtypeStruct((n_rows, DT_LANES), F32),
        ],
        scratch_shapes=[pltpu.VMEM((tm, D_MODEL), BF16)],
        compiler_params=pltpu.CompilerParams(
            dimension_semantics=("parallel", "arbitrary"), vmem_limit_bytes=VMEM_LIMIT),
    )(h, norm_w, w_main, w_dt)


def _mixer_kernel(conv_ref, z_ref, xbc_ref, qk_ref, vg_ref, dt_ref, ccos_ref, csin_ref,
                  bcos_ref, bsin_ref,
                  conva_ref, sconvw_ref, sconvb_ref, dtb_ref, alog_ref, dskip_ref, snorm_ref,
                  expand_ref,
                  ya_ref, yb_ref, yc_ref,
                  uhist_ref, xhist_ref, xs_ref, bc_ref, sstate_ref, rstate_ref, dmask_ref, kdec_ref,
                  qdec_ref):
    c = pl.program_id(0)
    li = lax.broadcasted_iota(jnp.int32, (CHUNK, CHUNK), 0)
    si = lax.broadcasted_iota(jnp.int32, (CHUNK, CHUNK), 1)
    causal = li >= si

    @pl.when(c == 0)
    def _():
        uhist_ref[0:HIST, :] = jnp.zeros((HIST, D_MODEL), F32)
        xhist_ref[0:HIST, :] = jnp.zeros((HIST, SSD_CONV_DIM), F32)
        sstate_ref[...] = jnp.zeros_like(sstate_ref)
        rstate_ref[...] = jnp.zeros_like(rstate_ref)
        rel = (li - si).astype(F32)
        row_w = lax.broadcasted_iota(jnp.int32, (CHUNK, RET_QK_DIM), 0).astype(F32)
        for h in range(RET_HEADS):
            log_gamma = math.log(1.0 - 2.0 ** (-5.0 - h))
            dmask_ref[h] = jnp.where(causal, jnp.exp(log_gamma * jnp.maximum(rel, 0.0)), 0.0)
            kdec_ref[h] = jnp.exp(log_gamma * (CHUNK - 1.0 - row_w))
            qdec_ref[h] = jnp.exp(log_gamma * (row_w + 1.0))

    def valid(shape):
        return (lax.broadcasted_iota(jnp.int32, shape, 0) + c * CHUNK) >= PAD

    strip_valid = valid((CHUNK, CONV_STRIP))

    def causal_conv(u, hist_ref, cs, taps_ref):
        k_taps = taps_ref.shape[0]
        hist_ref[HIST:HIST + CHUNK, cs] = u
        out = u * taps_ref[k_taps - 1:k_taps, cs]
        for t in range(k_taps - 1):
            off = HIST - (k_taps - 1) + t
            out = out + hist_ref[off:off + CHUNK, cs] * taps_ref[t:t + 1, cs]
        hist_ref[0:HIST, cs] = hist_ref[CHUNK:CHUNK + HIST, cs]
        return out

    for n in range(D_MODEL // CONV_STRIP):
        cs = slice(n * CONV_STRIP, (n + 1) * CONV_STRIP)
        b_gate = conv_ref[:, cs].astype(F32)
        c_gate = conv_ref[:, D_MODEL + cs.start:D_MODEL + cs.stop].astype(F32)
        xa = conv_ref[:, 2 * D_MODEL + cs.start:2 * D_MODEL + cs.stop].astype(F32)
        u = jnp.where(strip_valid, c_gate * xa, 0.0)
        ya_ref[:, cs] = (b_gate * causal_conv(u, uhist_ref, cs, conva_ref)).astype(ya_ref.dtype)

    for n in range(SSD_CONV_DIM // CONV_STRIP):
        cs = slice(n * CONV_STRIP, (n + 1) * CONV_STRIP)
        xin = jnp.where(strip_valid, xbc_ref[:, cs].astype(F32), 0.0)
        xc = _silu(causal_conv(xin, xhist_ref, cs, sconvw_ref) + sconvb_ref[:, cs])
        if cs.stop <= SSD_INNER:
            xs_ref[:, cs] = jnp.where(strip_valid, xc, 0.0)
        else:
            bc_ref[:, cs.start - SSD_INNER:cs.stop - SSD_INNER] = xc.astype(BF16)
    xs = xs_ref[...]
    bm = bc_ref[:, 0:SSD_GROUPS * SSD_STATE]
    cm = bc_ref[:, SSD_GROUPS * SSD_STATE:2 * SSD_GROUPS * SSD_STATE]

    dtv = _softplus(dt_ref[...] + dtb_ref[...])
    a = -jnp.exp(alog_ref[...]) * dtv
    tri = causal.astype(BF16)
    acs = sum(_dot(tri, part) for part in _split3(a))
    acs_t = acs.T
    acs_last = acs[CHUNK - 1:CHUNK, :]
    expand = expand_ref[...]

    def expand_heads(v):
        return sum(_dot(part, expand) for part in _split3(v))

    dt_x = expand_heads(dtv)
    eacs_x = jnp.exp(expand_heads(acs))
    dstate_x = jnp.exp(expand_heads(acs_last - acs))
    cdecay_x = eacs_x[CHUNK - 1:CHUNK, :]
    xdt = xs * dt_x
    xdt_b = xdt.astype(BF16)
    xw_b = (xdt * dstate_x).astype(BF16)

    hpg = SSD_HEADS // SSD_GROUPS
    groups = [slice(g * SSD_GROUP_WIDTH, (g + 1) * SSD_GROUP_WIDTH) for g in range(SSD_GROUPS)]
    ccs = [cm[:, g * SSD_STATE:(g + 1) * SSD_STATE] for g in range(SSD_GROUPS)]
    bcs = [bm[:, g * SSD_STATE:(g + 1) * SSD_STATE] for g in range(SSD_GROUPS)]
    cbs = [_dot_nt(cc, bc) for cc, bc in zip(ccs, bcs)]
    prevs = [sstate_ref[g] for g in range(SSD_GROUPS)]
    y_offs = [_dot(cc, prev.astype(BF16)) for cc, prev in zip(ccs, prevs)]
    new_states = [_dot_tn(bc, xw_b[:, gs]) for bc, gs in zip(bcs, groups)]
    for g in range(SSD_GROUPS):
        sstate_ref[g] = prevs[g] * cdecay_x[:, groups[g]] + new_states[g]
    yd = []
    for h in range(SSD_HEADS):
        diff = acs[:, h:h + 1] - acs_t[h:h + 1, :]
        seg = jnp.exp(jnp.where(causal, diff, -jnp.inf))
        hs = slice(h * SSD_HEAD_DIM, (h + 1) * SSD_HEAD_DIM)
        yd.append(_dot((cbs[h // hpg] * seg).astype(BF16), xdt_b[:, hs]))
    y = jnp.concatenate(yd, axis=-1) + jnp.concatenate(y_offs, axis=-1) * eacs_x + xs * dskip_ref[...]
    y = y * _silu(z_ref[...].astype(F32))
    yn = []
    for g in range(SSD_GROUPS):
        yg = y[:, g * SSD_GROUP_WIDTH:(g + 1) * SSD_GROUP_WIDTH]
        yn.append(yg * lax.rsqrt(jnp.mean(yg * yg, axis=-1, keepdims=True) + EPS))
    yb_ref[...] = (jnp.concatenate(yn, axis=-1) * snorm_ref[...]).astype(yb_ref.dtype)

    cc_, sc_ = ccos_ref[0], csin_ref[0]
    bcos, bsin = bcos_ref[...], bsin_ref[...]
    cos = cc_ * bcos - sc_ * bsin
    sin = sc_ * bcos + cc_ * bsin
    half = RET_QK_DIM // 2
    ko = RET_HEADS * RET_QK_DIM
    go = RET_HEADS * RET_V_DIM
    v_valid = valid((CHUNK, RET_V_DIM))

    def rotated(off):
        x1 = qk_ref[:, off:off + half].astype(F32)
        x2 = qk_ref[:, off + half:off + RET_QK_DIM].astype(F32)
        return jnp.concatenate([x1 * cos - x2 * sin, x1 * sin + x2 * cos], axis=-1)

    qrs, krs, vrs, prevs = [], [], [], []
    for h in range(RET_HEADS):
        qrs.append(rotated(h * RET_QK_DIM).astype(BF16))
        krs.append(rotated(ko + h * RET_QK_DIM) * (RET_QK_DIM ** -0.5))
        vrs.append(jnp.where(v_valid, vg_ref[:, h * RET_V_DIM:(h + 1) * RET_V_DIM].astype(F32), 0.0).astype(BF16))
        prevs.append(rstate_ref[h])
    scores = [_dot_nt(qr, kr.astype(BF16)) for qr, kr in zip(qrs, krs)]
    y_crs = [_dot(qr, prev.astype(BF16)) for qr, prev in zip(qrs, prevs)]
    kvs = [_dot_tn((krs[h] * kdec_ref[h]).astype(BF16), vrs[h]) for h in range(RET_HEADS)]
    for h in range(RET_HEADS):
        log_gamma = math.log(1.0 - 2.0 ** (-5.0 - h))
        rstate_ref[h] = prevs[h] * math.exp(log_gamma * CHUNK) + kvs[h]
    y_ins = [_dot((scores[h] * dmask_ref[h]).astype(BF16), vrs[h]) for h in range(RET_HEADS)]
    yc = []
    for h in range(RET_HEADS):
        gate = vg_ref[:, go + h * RET_V_DIM:go + (h + 1) * RET_V_DIM].astype(F32)
        yh = y_ins[h] + y_crs[h] * qdec_ref[h]
        mu = jnp.mean(yh, axis=-1, keepdims=True)
        yh = yh - mu
        var = jnp.mean(yh * yh, axis=-1, keepdims=True)
        yc.append(yh * lax.rsqrt(var + EPS) * _silu(gate))
    yc_ref[...] = jnp.concatenate(yc, axis=-1).astype(yc_ref.dtype)


def _mixers(proj, dt_raw, rope, conv_a, sconv_w, sconv_b, dt_bias, a_log, d_skip, snorm, expand):
    n_rows = proj.shape[0]
    nc = n_rows // CHUNK
    chunk_cos, chunk_sin, base_cos, base_sin = rope

    def col(width, start):
        assert start % width == 0
        return pl.BlockSpec((CHUNK, width), lambda c, s=start // width: (c, s))

    def full(arr):
        return pl.BlockSpec(arr.shape, lambda c: (0,) * arr.ndim)

    row_block = pl.BlockSpec((CHUNK, D_MODEL), lambda c: (c, 0))
    chunk_row = pl.BlockSpec((1, 1, RET_QK_DIM // 2), lambda c: (c, 0, 0))
    weights = (base_cos, base_sin, conv_a, sconv_w, sconv_b, dt_bias, a_log, d_skip, snorm, expand)
    return pl.pallas_call(
        _mixer_kernel,
        grid=(nc,),
        in_specs=[
            col(3 * D_MODEL, COL_CONV), col(D_MODEL, COL_SSD_Z), col(SSD_CONV_DIM, COL_SSD_XBC),
            col(2048, COL_RET_QK), col(2048, COL_RET_VG),
            pl.BlockSpec((CHUNK, DT_LANES), lambda c: (c, 0)), chunk_row, chunk_row,
        ] + [full(w) for w in weights],
        out_specs=[row_block, row_block, row_block],
        out_shape=[jax.ShapeDtypeStruct((n_rows, D_MODEL), BF16)] * 3,
        scratch_shapes=[
            pltpu.VMEM((HIST + CHUNK, D_MODEL), F32),
            pltpu.VMEM((HIST + CHUNK, SSD_CONV_DIM), F32),
            pltpu.VMEM((CHUNK, SSD_INNER), F32),
            pltpu.VMEM((CHUNK, 2 * SSD_GROUPS * SSD_STATE), BF16),
            pltpu.VMEM((SSD_GROUPS, SSD_STATE, SSD_GROUP_WIDTH), F32),
            pltpu.VMEM((RET_HEADS, RET_QK_DIM, RET_V_DIM), F32),
            pltpu.VMEM((RET_HEADS, CHUNK, CHUNK), F32),
            pltpu.VMEM((RET_HEADS, CHUNK, RET_QK_DIM), F32),
            pltpu.VMEM((RET_HEADS, CHUNK, RET_QK_DIM), F32),
        ],
        compiler_params=pltpu.CompilerParams(
            dimension_semantics=("arbitrary",), vmem_limit_bytes=VMEM_LIMIT),
    )(proj, proj, proj, proj, proj, dt_raw, chunk_cos, chunk_sin, *weights)


def _sb_kernel(q_ref, k_ref, v_ref, proj_hbm, o_ref, kbuf, vbuf, acc_ref, run_ref, sem):
    i = pl.program_id(0)
    kbuf[i % SB_RING] = k_ref[...]
    vbuf[i % SB_RING] = v_ref[...]
    acc_ref[...] = jnp.zeros_like(acc_ref)
    run_ref[...] = jnp.zeros_like(run_ref)
    scale = SB_HEAD_DIM ** -0.5
    heads = [slice(h * SB_HEAD_DIM, (h + 1) * SB_HEAD_DIM) for h in range(SB_HEADS)]
    sfx_r = lax.broadcasted_iota(jnp.int32, (CHUNK, CHUNK), 0)
    sfx_c = lax.broadcasted_iota(jnp.int32, (CHUNK, CHUNK), 1)
    suffix = (sfx_r >= sfx_c).astype(BF16)

    def fetch(j):
        far = (i - j) >= SB_RING

        @pl.when(far)
        def _():
            rows_j = pl.ds(pl.multiple_of(j * CHUNK, CHUNK), CHUNK)
            copies = [
                pltpu.make_async_copy(proj_hbm.at[rows_j, pl.ds(COL_SB_K, D_MODEL)], kbuf.at[SB_RING], sem.at[0]),
                pltpu.make_async_copy(proj_hbm.at[rows_j, pl.ds(COL_SB_V, D_MODEL)], vbuf.at[SB_RING], sem.at[1]),
            ]
            for cp in copies:
                cp.start()
            for cp in copies:
                cp.wait()

        return jnp.where(far, SB_RING, j % SB_RING)

    def visit(j, slot, causal, pad, n):
        rows = lax.broadcasted_iota(jnp.int32, (n, CHUNK), 0)
        cols = lax.broadcasted_iota(jnp.int32, (n, CHUNK), 1)
        mask = None
        if causal:
            mask = cols < rows
        if pad:
            in_seq = (cols + j * CHUNK) >= PAD
            mask = in_seq if mask is None else mask & in_seq
        zs = [_dot_nt(q_ref[0:n, hs], kbuf[slot, :, hs]) * scale for hs in heads]
        csums = []
        for z in zs:
            sp = jnp.maximum(z, 0.0) + jnp.log(1.0 + jnp.exp(-jnp.abs(z)))
            if mask is not None:
                sp = jnp.where(mask, sp, 0.0)
            csums.append(_dot(sp.astype(BF16), suffix))
        run_min = None
        outs = []
        for h, (z, csum) in enumerate(zip(zs, csums)):
            run = run_ref[h, 0:n, :]
            w = jnp.exp(z - csum - run)
            if mask is not None:
                w = jnp.where(mask, w, 0.0)
            outs.append(_dot(w.astype(BF16), vbuf[slot, :, heads[h]]))
            run = run + csum[:, 0:1]
            run_ref[h, 0:n, :] = run
            run_min = run if run_min is None else jnp.minimum(run_min, run)
        acc_ref[0:n, :] += jnp.concatenate(outs, axis=-1)
        top = jnp.min(run_min[0:SB_TOP])
        rest = jnp.min(run_min[SB_TOP:n]) if n > SB_TOP else None
        return top, rest

    def step(j, top, rest, pad):
        slot = fetch(j)

        def top_rows():
            return visit(j, slot, False, pad, SB_TOP)[0], rest

        def all_rows():
            return visit(j, slot, False, pad, CHUNK)

        return lax.cond(rest >= SB_UNDERFLOW, top_rows, all_rows)

    top, rest = visit(i, i % SB_RING, True, True, CHUNK)

    def body(carry):
        j, top, rest = carry
        top, rest = step(j, top, rest, False)
        return j - 1, top, rest

    def cond(carry):
        j, top, rest = carry
        return (j >= 1) & (jnp.minimum(top, rest) < SB_UNDERFLOW)

    j, top, rest = lax.while_loop(cond, body, (i - 1, top, rest))

    @pl.when((j == 0) & (jnp.minimum(top, rest) < SB_UNDERFLOW))
    def _():
        step(0, top, rest, True)

    o_ref[...] = acc_ref[...].astype(o_ref.dtype)


def _stick_breaking(proj):
    n_rows = proj.shape[0]

    def col(start):
        assert start % D_MODEL == 0
        return pl.BlockSpec((CHUNK, D_MODEL), lambda i, s=start // D_MODEL: (i, s))

    return pl.pallas_call(
        _sb_kernel,
        grid=(n_rows // CHUNK,),
        in_specs=[col(COL_SB_Q), col(COL_SB_K), col(COL_SB_V), pl.BlockSpec(memory_space=pl.ANY)],
        out_specs=pl.BlockSpec((CHUNK, D_MODEL), lambda i: (i, 0)),
        out_shape=jax.ShapeDtypeStruct((n_rows, D_MODEL), BF16),
        scratch_shapes=[
            pltpu.VMEM((SB_RING + 1, CHUNK, D_MODEL), BF16),
            pltpu.VMEM((SB_RING + 1, CHUNK, D_MODEL), BF16),
            pltpu.VMEM((CHUNK, D_MODEL), F32),
            pltpu.VMEM((SB_HEADS, CHUNK, 1), F32),
            pltpu.SemaphoreType.DMA((2,)),
        ],
        compiler_params=pltpu.CompilerParams(
            dimension_semantics=("arbitrary",), vmem_limit_bytes=VMEM_LIMIT),
    )(proj, proj, proj, proj)


def _merge_kernel(ya_ref, yb_ref, yc_ref, yd_ref, ga_ref, gb_ref, gc_ref, gd_ref, h_ref,
                  wb_ref, wo_ref, nw_ref, o_ref):
    merged = None
    branches = ((ya_ref, ga_ref), (yb_ref, gb_ref), (yc_ref, gc_ref), (yd_ref, gd_ref))
    for n, (y_ref, g_ref) in enumerate(branches):
        up = _dot(y_ref[...], wb_ref[n])
        gate = _sigmoid(g_ref[...].astype(F32))
        merged = gate * up if merged is None else merged + gate * up
    mix = _dot(merged.astype(BF16), wo_ref[...])
    o_ref[...] = h_ref[...] + _rmsnorm(mix, nw_ref[...])


def _merge(ya, yb, yc, yd, proj, h, w_branch, w_out, norm_w, layer):
    n_rows = h.shape[0]
    tm = _row_tile(n_rows, 384)
    row_block = pl.BlockSpec((tm, D_MODEL), lambda i: (i, 0))
    return pl.pallas_call(
        _merge_kernel,
        grid=(n_rows // tm,),
        in_specs=[
            row_block, row_block, row_block, row_block,
        ] + [
            pl.BlockSpec((tm, D_MODEL), lambda i, s=COL_GATE // D_MODEL + n: (i, s)) for n in range(N_BRANCH)
        ] + [
            row_block,
            pl.BlockSpec((None, N_BRANCH, D_MODEL, D_MODEL), lambda i: (layer, 0, 0, 0),
                         pipeline_mode=pl.Buffered(1)),
            pl.BlockSpec((None, D_MODEL, D_MODEL), lambda i: (layer, 0, 0), pipeline_mode=pl.Buffered(1)),
            pl.BlockSpec((1, D_MODEL), lambda i: (0, 0)),
        ],
        out_specs=row_block,
        out_shape=jax.ShapeDtypeStruct((n_rows, D_MODEL), F32),
        compiler_params=pltpu.CompilerParams(
            dimension_semantics=("parallel",), vmem_limit_bytes=VMEM_LIMIT),
    )(ya, yb, yc, yd, proj, proj, proj, proj, h, w_branch, w_out, norm_w)


def _ffn_kernel(*refs, n_pieces):
    h_refs = refs[:n_pieces]
    w1_ref, w2_ref, n1_ref, n2_ref, o_ref = refs[n_pieces:]
    h = h_refs[0][...] if n_pieces == 1 else jnp.concatenate([r[...] for r in h_refs], axis=0)
    xn = _rmsnorm(h, n1_ref[...]).astype(BF16)
    f = _dot(xn, w1_ref[...])
    act = (_silu(f[:, 0:D_FF]) * f[:, D_FF:2 * D_FF]).astype(BF16)
    o_ref[...] = h + _rmsnorm(_dot(act, w2_ref[...]), n2_ref[...])


def _ffn(h, w1, w2, n1, n2, layer, drop_first_chunk):
    n_rows = h.shape[0]
    if drop_first_chunk:
        n_out = n_rows - CHUNK
        tm = max(t for t in range(CHUNK, FFN_FINAL_TILE + 1, CHUNK) if n_out % t == 0)
        n_pieces = tm // CHUNK
        h_specs = [pl.BlockSpec((CHUNK, D_MODEL), lambda i, k=k: (i * n_pieces + 1 + k, 0))
                   for k in range(n_pieces)]
    else:
        n_out = n_rows
        tm = _row_tile(n_rows, 384)
        n_pieces = 1
        h_specs = [pl.BlockSpec((tm, D_MODEL), lambda i: (i, 0))]
    resident = pl.Buffered(1)
    return pl.pallas_call(
        functools.partial(_ffn_kernel, n_pieces=n_pieces),
        grid=(n_out // tm,),
        in_specs=h_specs + [
            pl.BlockSpec((None, D_MODEL, 2 * D_FF), lambda i: (layer, 0, 0), pipeline_mode=resident),
            pl.BlockSpec((None, D_FF, D_MODEL), lambda i: (layer, 0, 0), pipeline_mode=resident),
            pl.BlockSpec((1, D_MODEL), lambda i: (0, 0)),
            pl.BlockSpec((1, D_MODEL), lambda i: (0, 0)),
        ],
        out_specs=pl.BlockSpec((tm, D_MODEL), lambda i: (i, 0)),
        out_shape=jax.ShapeDtypeStruct((n_out, D_MODEL), F32),
        compiler_params=pltpu.CompilerParams(
            dimension_semantics=("parallel",), vmem_limit_bytes=VMEM_LIMIT),
    )(*([h] * n_pieces), w1, w2, n1, n2)


def _rope_tables(n_chunks):
    half = RET_QK_DIM // 2
    inv = ROPE_BASE ** (-np.arange(half, dtype=np.float64) / half)
    chunk_ang = (np.arange(n_chunks, dtype=np.float64) * CHUNK)[:, None, None] * inv[None, None, :]
    base_ang = np.arange(CHUNK, dtype=np.float64)[:, None] * inv[None, :]
    return tuple(jnp.asarray(t, F32) for t in
                 (np.cos(chunk_ang), np.sin(chunk_ang), np.cos(base_ang), np.sin(base_ang)))


def _head_expand_matrix():
    e = np.zeros((DT_LANES, SSD_INNER), np.float32)
    for h in range(SSD_HEADS):
        e[h, h * SSD_HEAD_DIM:(h + 1) * SSD_HEAD_DIM] = 1.0
    return jnp.asarray(e, BF16)


def _pad_lanes(v):
    return jnp.pad(v.astype(F32), (0, DT_LANES - v.shape[0]))[None, :]


def kernel(x, meta, w_in, conv_a, ssd_conv_w, ssd_conv_b, ssd_dt_bias, ssd_a_log, ssd_d, ssd_norm,
           w_branch, w_out, w_ffn_in, w_ffn_out, norm_mix_pre, norm_mix_post, norm_ffn_pre,
           norm_ffn_post):
    batch, seq, _ = x.shape
    assert batch == 1 and seq % CHUNK == 0
    depth = w_in.shape[0]
    n_rows = seq + CHUNK
    h = jnp.concatenate([jnp.zeros((PAD, D_MODEL), F32), meta.astype(F32), x[0].astype(F32)], axis=0)
    rope = _rope_tables(n_rows // CHUNK)
    expand = _head_expand_matrix()
    dt_end = DT_COL_START + SSD_HEADS
    dt_end = DT_COL_START + SSD_HEADS
    w_t = jnp.swapaxes(w_in, 1, 2)
    w_main = jnp.concatenate([w_t[:, :DT_COL_START], w_t[:, dt_end:]], axis=1).astype(BF16)
    w_dt = jnp.pad(w_t[:, DT_COL_START:dt_end], ((0, 0), (0, DT_LANES - SSD_HEADS), (0, 0))).astype(BF16)
    w_branch_b, w_out_b = w_branch.astype(BF16), w_out.astype(BF16)
    w_ffn_in_b, w_ffn_out_b = w_ffn_in.astype(BF16), w_ffn_out.astype(BF16)
    for l in range(depth):
        proj, dt_raw = _inproj(h, norm_mix_pre[l][None, :], w_main, w_dt, l)
        ya, yb, yc = _mixers(
            proj, dt_raw, rope, conv_a[l], ssd_conv_w[l], ssd_conv_b[l][None, :],
            _pad_lanes(ssd_dt_bias[l]), _pad_lanes(ssd_a_log[l]),
            jnp.repeat(ssd_d[l].astype(F32), SSD_HEAD_DIM)[None, :], ssd_norm[l][None, :], expand)
        yd = _stick_breaking(proj)
        h = _merge(ya, yb, yc, yd, proj, h, w_branch_b, w_out_b, norm_mix_post[l][None, :], l)
        h = _ffn(h, w_ffn_in_b, w_ffn_out_b, norm_ffn_pre[l][None, :], norm_ffn_post[l][None, :], l,
                 drop_first_chunk=(l == depth - 1))
    return h[None].astype(x.dtype)
```

```python
import functools
import math

import numpy as np
import jax
import jax.numpy as jnp
from jax import lax
from jax.experimental import pallas as pl
from jax.experimental.pallas import tpu as pltpu

F32 = jnp.float32
BF16 = jnp.bfloat16

D_MODEL = 1024
N_META = 16
CHUNK = 128
PAD = CHUNK - N_META
EPS = 1e-6

SSD_HEAD_DIM = 64
SSD_HEADS = 16
SSD_INNER = 1024
SSD_GROUPS = 4
SSD_STATE = 128
SSD_CONV_K = 4
SSD_CONV_DIM = SSD_INNER + 2 * SSD_GROUPS * SSD_STATE
SSD_GROUP_WIDTH = SSD_INNER // SSD_GROUPS
CONV_A_K = 3
RET_HEADS = 4
RET_QK_DIM = 256
RET_V_DIM = 256
ROPE_BASE = 10000.0
SB_HEADS = 8
SB_HEAD_DIM = 128
N_BRANCH = 4
D_FF = 2816

COL_CONV = 0
COL_SSD_Z = 3072
COL_SSD_XBC = 4096
COL_RET_QK = 6144
COL_RET_VG = 8192
COL_SB_Q = 10240
COL_SB_K = 11264
COL_SB_V = 12288
COL_GATE = 13312
PROJ_WIDTH = 17408
DT_COL_START = 6144
DT_LANES = 128
CONV_STRIP = 256
FFN_FINAL_TILE = 512

HIST = 8
SB_UNDERFLOW = 104.0
SB_RING = 4
SB_TOP = 32
VMEM_LIMIT = 56 * 1024 * 1024


def _rmsnorm(x, w):
    return x * lax.rsqrt(jnp.mean(x * x, axis=-1, keepdims=True) + EPS) * w


def _sigmoid(x):
    return 1.0 / (1.0 + jnp.exp(-x))


def _silu(x):
    return x * _sigmoid(x)


def _softplus(x):
    return jnp.maximum(x, 0.0) + jnp.log1p(jnp.exp(-jnp.abs(x)))


def _split3(x):
    hi = x.astype(BF16)
    r = x - hi.astype(F32)
    mid = r.astype(BF16)
    lo = (r - mid.astype(F32)).astype(BF16)
    return hi, mid, lo


def _dot(a, b):
    return jnp.dot(a, b, preferred_element_type=F32)


def _dot_nt(a, b):
    return lax.dot_general(a, b, (((1,), (1,)), ((), ())), preferred_element_type=F32)


def _dot_tn(a, b):
    return lax.dot_general(a, b, (((0,), (0,)), ((), ())), preferred_element_type=F32)


def _row_tile(n_rows, target):
    best = 16
    for t in range(16, min(n_rows, target) + 1, 16):
        if n_rows % t == 0:
            best = t
    return best


def _inproj_kernel(x_ref, nw_ref, w_ref, wdt_ref, o_ref, dt_ref, xn_ref):
    @pl.when(pl.program_id(1) == 0)
    def _():
        xn = _rmsnorm(x_ref[...], nw_ref[...]).astype(BF16)
        xn_ref[...] = xn
        dt_ref[...] = _dot(xn, wdt_ref[...])

    o_ref[...] = _dot(xn_ref[...], w_ref[...]).astype(o_ref.dtype)


def _inproj(h, norm_w, w_main, w_dt, layer):
    n_rows = h.shape[0]
    tm = _row_tile(n_rows, 1376)
    tn = 1024
    return pl.pallas_call(
        _inproj_kernel,
        grid=(n_rows // tm, PROJ_WIDTH // tn),
        in_specs=[
            pl.BlockSpec((tm, D_MODEL), lambda i, j: (i, 0)),
            pl.BlockSpec((1, D_MODEL), lambda i, j: (0, 0)),
            pl.BlockSpec((None, D_MODEL, tn), lambda i, j: (layer, 0, j)),
            pl.BlockSpec((None, D_MODEL, DT_LANES), lambda i, j: (layer, 0, 0)),
        ],
        out_specs=[
            pl.BlockSpec((tm, tn), lambda i, j: (i, j)),
            pl.BlockSpec((tm, DT_LANES), lambda i, j: (i, 0)),
        ],
        out_shape=[
            jax.ShapeDtypeStruct((n_rows, PROJ_WIDTH), BF16),
            jax.ShapeDtypeStruct((n_rows, DT_LANES), F32),
        ],
        scratch_shapes=[pltpu.VMEM((tm, D_MODEL), BF16)],
        compiler_params=pltpu.CompilerParams(
            dimension_semantics=("parallel", "arbitrary"), vmem_limit_bytes=VMEM_LIMIT),
    )(h, norm_w, w_main, w_dt)


def _mixer_kernel(conv_ref, z_ref, xbc_ref, qk_ref, vg_ref, dt_ref, ccos_ref, csin_ref,
                  bcos_ref, bsin_ref,
                  conva_ref, sconvw_ref, sconvb_ref, dtb_ref, alog_ref, dskip_ref, snorm_ref,
                  expand_ref,
                  ya_ref, yb_ref, yc_ref,
                  uhist_ref, xhist_ref, xs_ref, bc_ref, sstate_ref, rstate_ref, dmask_ref, kdec_ref,
                  qdec_ref):
    c = pl.program_id(0)
    li = lax.broadcasted_iota(jnp.int32, (CHUNK, CHUNK), 0)
    si = lax.broadcasted_iota(jnp.int32, (CHUNK, CHUNK), 1)
    causal = li >= si

    @pl.when(c == 0)
    def _():
        uhist_ref[0:HIST, :] = jnp.zeros((HIST, D_MODEL), F32)
        xhist_ref[0:HIST, :] = jnp.zeros((HIST, SSD_CONV_DIM), F32)
        sstate_ref[...] = jnp.zeros_like(sstate_ref)
        rstate_ref[...] = jnp.zeros_like(rstate_ref)
        rel = (li - si).astype(F32)
        row_w = lax.broadcasted_iota(jnp.int32, (CHUNK, RET_QK_DIM), 0).astype(F32)
        for h in range(RET_HEADS):
            log_gamma = math.log(1.0 - 2.0 ** (-5.0 - h))
            dmask_ref[h] = jnp.where(causal, jnp.exp(log_gamma * jnp.maximum(rel, 0.0)), 0.0)
            kdec_ref[h] = jnp.exp(log_gamma * (CHUNK - 1.0 - row_w))
            qdec_ref[h] = jnp.exp(log_gamma * (row_w + 1.0))

    def valid(shape):
        return (lax.broadcasted_iota(jnp.int32, shape, 0) + c * CHUNK) >= PAD

    strip_valid = valid((CHUNK, CONV_STRIP))

    def causal_conv(u, hist_ref, cs, taps_ref):
        k_taps = taps_ref.shape[0]
        hist_ref[HIST:HIST + CHUNK, cs] = u
        out = u * taps_ref[k_taps - 1:k_taps, cs]
        for t in range(k_taps - 1):
            off = HIST - (k_taps - 1) + t
            out = out + hist_ref[off:off + CHUNK, cs] * taps_ref[t:t + 1, cs]
        hist_ref[0:HIST, cs] = hist_ref[CHUNK:CHUNK + HIST, cs]
        return out

    for n in range(D_MODEL // CONV_STRIP):
        cs = slice(n * CONV_STRIP, (n + 1) * CONV_STRIP)
        b_gate = conv_ref[:, cs].astype(F32)
        c_gate = conv_ref[:, D_MODEL + cs.start:D_MODEL + cs.stop].astype(F32)
        xa = conv_ref[:, 2 * D_MODEL + cs.start:2 * D_MODEL + cs.stop].astype(F32)
        u = jnp.where(strip_valid, c_gate * xa, 0.0)
        ya_ref[:, cs] = (b_gate * causal_conv(u, uhist_ref, cs, conva_ref)).astype(ya_ref.dtype)

    for n in range(SSD_CONV_DIM // CONV_STRIP):
        cs = slice(n * CONV_STRIP, (n + 1) * CONV_STRIP)
        xin = jnp.where(strip_valid, xbc_ref[:, cs].astype(F32), 0.0)
        xc = _silu(causal_conv(xin, xhist_ref, cs, sconvw_ref) + sconvb_ref[:, cs])
        if cs.stop <= SSD_INNER:
            xs_ref[:, cs] = jnp.where(strip_valid, xc, 0.0)
        else:
            bc_ref[:, cs.start - SSD_INNER:cs.stop - SSD_INNER] = xc.astype(BF16)
    xs = xs_ref[...]
    bm = bc_ref[:, 0:SSD_GROUPS * SSD_STATE]
    cm = bc_ref[:, SSD_GROUPS * SSD_STATE:2 * SSD_GROUPS * SSD_STATE]

    dtv = _softplus(dt_ref[...] + dtb_ref[...])
    a = -jnp.exp(alog_ref[...]) * dtv
    tri = causal.astype(BF16)
    acs = sum(_dot(tri, part) for part in _split3(a))
    acs_t = acs.T
    acs_last = acs[CHUNK - 1:CHUNK, :]
    expand = expand_ref[...]

    def expand_heads(v):
        return sum(_dot(part, expand) for part in _split3(v))

    dt_x = expand_heads(dtv)
    eacs_x = jnp.exp(expand_heads(acs))
    dstate_x = jnp.exp(expand_heads(acs_last - acs))
    cdecay_x = eacs_x[CHUNK - 1:CHUNK, :]
    xdt = xs * dt_x
    xdt_b = xdt.astype(BF16)
    xw_b = (xdt * dstate_x).astype(BF16)

    hpg = SSD_HEADS // SSD_GROUPS
    groups = [slice(g * SSD_GROUP_WIDTH, (g + 1) * SSD_GROUP_WIDTH) for g in range(SSD_GROUPS)]
    ccs = [cm[:, g * SSD_STATE:(g + 1) * SSD_STATE] for g in range(SSD_GROUPS)]
    bcs = [bm[:, g * SSD_STATE:(g + 1) * SSD_STATE] for g in range(SSD_GROUPS)]
    cbs = [_dot_nt(cc, bc) for cc, bc in zip(ccs, bcs)]
    prevs = [sstate_ref[g] for g in range(SSD_GROUPS)]
    y_offs = [_dot(cc, prev.astype(BF16)) for cc, prev in zip(ccs, prevs)]
    new_states = [_dot_tn(bc, xw_b[:, gs]) for bc, gs in zip(bcs, groups)]
    for g in range(SSD_GROUPS):
        sstate_ref[g] = prevs[g] * cdecay_x[:, groups[g]] + new_states[g]
    yd = []
    for h in range(SSD_HEADS):
        diff = acs[:, h:h + 1] - acs_t[h:h + 1, :]
        seg = jnp.exp(jnp.where(causal, diff, -jnp.inf))
        hs = slice(h * SSD_HEAD_DIM, (h + 1) * SSD_HEAD_DIM)
        yd.append(_dot((cbs[h // hpg] * seg).astype(BF16), xdt_b[:, hs]))
    y = jnp.concatenate(yd, axis=-1) + jnp.concatenate(y_offs, axis=-1) * eacs_x + xs * dskip_ref[...]
    y = y * _silu(z_ref[...].astype(F32))
    yn = []
    for g in range(SSD_GROUPS):
        yg = y[:, g * SSD_GROUP_WIDTH:(g + 1) * SSD_GROUP_WIDTH]
        yn.append(yg * lax.rsqrt(jnp.mean(yg * yg, axis=-1, keepdims=True) + EPS))
    yb_ref[...] = (jnp.concatenate(yn, axis=-1) * snorm_ref[...]).astype(yb_ref.dtype)

    cc_, sc_ = ccos_ref[0], csin_ref[0]
    bcos, bsin = bcos_ref[...], bsin_ref[...]
    cos = cc_ * bcos - sc_ * bsin
    sin = sc_ * bcos + cc_ * bsin
    half = RET_QK_DIM // 2
    ko = RET_HEADS * RET_QK_DIM
    go = RET_HEADS * RET_V_DIM
    v_valid = valid((CHUNK, RET_V_DIM))

    def rotated(off):
        x1 = qk_ref[:, off:off + half].astype(F32)
        x2 = qk_ref[:, off + half:off + RET_QK_DIM].astype(F32)
        return jnp.concatenate([x1 * cos - x2 * sin, x1 * sin + x2 * cos], axis=-1)

    qrs, krs, vrs, prevs = [], [], [], []
    for h in range(RET_HEADS):
        qrs.append(rotated(h * RET_QK_DIM).astype(BF16))
        krs.append(rotated(ko + h * RET_QK_DIM) * (RET_QK_DIM ** -0.5))
        vrs.append(jnp.where(v_valid, vg_ref[:, h * RET_V_DIM:(h + 1) * RET_V_DIM].astype(F32), 0.0).astype(BF16))
        prevs.append(rstate_ref[h])
    scores = [_dot_nt(qr, kr.astype(BF16)) for qr, kr in zip(qrs, krs)]
    y_crs = [_dot(qr, prev.astype(BF16)) for qr, prev in zip(qrs, prevs)]
    kvs = [_dot_tn((krs[h] * kdec_ref[h]).astype(BF16), vrs[h]) for h in range(RET_HEADS)]
    for h in range(RET_HEADS):
        log_gamma = math.log(1.0 - 2.0 ** (-5.0 - h))
        rstate_ref[h] = prevs[h] * math.exp(log_gamma * CHUNK) + kvs[h]
    y_ins = [_dot((scores[h] * dmask_ref[h]).astype(BF16), vrs[h]) for h in range(RET_HEADS)]
    yc = []
    for h in range(RET_HEADS):
        gate = vg_ref[:, go + h * RET_V_DIM:go + (h + 1) * RET_V_DIM].astype(F32)
        yh = y_ins[h] + y_crs[h] * qdec_ref[h]
        mu = jnp.mean(yh, axis=-1, keepdims=True)
        yh = yh - mu
        var = jnp.mean(yh * yh, axis=-1, keepdims=True)
        yc.append(yh * lax.rsqrt(var + EPS) * _silu(gate))
    yc_ref[...] = jnp.concatenate(yc, axis=-1).astype(yc_ref.dtype)


def _mixers(proj, dt_raw, rope, conv_a, sconv_w, sconv_b, dt_bias, a_log, d_skip, snorm, expand):
    n_rows = proj.shape[0]
    nc = n_rows // CHUNK
    chunk_cos, chunk_sin, base_cos, base_sin = rope

    def col(width, start):
        assert start % width == 0
        return pl.BlockSpec((CHUNK, width), lambda c, s=start // width: (c, s))

    def full(arr):
        return pl.BlockSpec(arr.shape, lambda c: (0,) * arr.ndim)

    row_block = pl.BlockSpec((CHUNK, D_MODEL), lambda c: (c, 0))
    chunk_row = pl.BlockSpec((1, 1, RET_QK_DIM // 2), lambda c: (c, 0, 0))
    weights = (base_cos, base_sin, conv_a, sconv_w, sconv_b, dt_bias, a_log, d_skip, snorm, expand)
    return pl.pallas_call(
        _mixer_kernel,
        grid=(nc,),
        in_specs=[
            col(3 * D_MODEL, COL_CONV), col(D_MODEL, COL_SSD_Z), col(SSD_CONV_DIM, COL_SSD_XBC),
            col(2048, COL_RET_QK), col(2048, COL_RET_VG),
            pl.BlockSpec((CHUNK, DT_LANES), lambda c: (c, 0)), chunk_row, chunk_row,
        ] + [full(w) for w in weights],
        out_specs=[row_block, row_block, row_block],
        out_shape=[jax.ShapeDtypeStruct((n_rows, D_MODEL), BF16)] * 3,
        scratch_shapes=[
            pltpu.VMEM((HIST + CHUNK, D_MODEL), F32),
            pltpu.VMEM((HIST + CHUNK, SSD_CONV_DIM), F32),
            pltpu.VMEM((CHUNK, SSD_INNER), F32),
            pltpu.VMEM((CHUNK, 2 * SSD_GROUPS * SSD_STATE), BF16),
            pltpu.VMEM((SSD_GROUPS, SSD_STATE, SSD_GROUP_WIDTH), F32),
            pltpu.VMEM((RET_HEADS, RET_QK_DIM, RET_V_DIM), F32),
            pltpu.VMEM((RET_HEADS, CHUNK, CHUNK), F32),
            pltpu.VMEM((RET_HEADS, CHUNK, RET_QK_DIM), F32),
            pltpu.VMEM((RET_HEADS, CHUNK, RET_QK_DIM), F32),
        ],
        compiler_params=pltpu.CompilerParams(
            dimension_semantics=("arbitrary",), vmem_limit_bytes=VMEM_LIMIT),
    )(proj, proj, proj, proj, proj, dt_raw, chunk_cos, chunk_sin, *weights)


def _sb_kernel(q_ref, k_ref, v_ref, proj_hbm, o_ref, kbuf, vbuf, acc_ref, run_ref, sem):
    i = pl.program_id(0)
    kbuf[i % SB_RING] = k_ref[...]
    vbuf[i % SB_RING] = v_ref[...]
    acc_ref[...] = jnp.zeros_like(acc_ref)
    run_ref[...] = jnp.zeros_like(run_ref)
    scale = SB_HEAD_DIM ** -0.5
    heads = [slice(h * SB_HEAD_DIM, (h + 1) * SB_HEAD_DIM) for h in range(SB_HEADS)]
    sfx_r = lax.broadcasted_iota(jnp.int32, (CHUNK, CHUNK), 0)
    sfx_c = lax.broadcasted_iota(jnp.int32, (CHUNK, CHUNK), 1)
    suffix = (sfx_r >= sfx_c).astype(BF16)

    def fetch(j):
        far = (i - j) >= SB_RING

        @pl.when(far)
        def _():
            rows_j = pl.ds(pl.multiple_of(j * CHUNK, CHUNK), CHUNK)
            copies = [
                pltpu.make_async_copy(proj_hbm.at[rows_j, pl.ds(COL_SB_K, D_MODEL)], kbuf.at[SB_RING], sem.at[0]),
                pltpu.make_async_copy(proj_hbm.at[rows_j, pl.ds(COL_SB_V, D_MODEL)], vbuf.at[SB_RING], sem.at[1]),
            ]
            for cp in copies:
                cp.start()
            for cp in copies:
                cp.wait()

        return jnp.where(far, SB_RING, j % SB_RING)

    def visit(j, slot, causal, pad, n):
        rows = lax.broadcasted_iota(jnp.int32, (n, CHUNK), 0)
        cols = lax.broadcasted_iota(jnp.int32, (n, CHUNK), 1)
        mask = None
        if causal:
            mask = cols < rows
        if pad:
            in_seq = (cols + j * CHUNK) >= PAD
            mask = in_seq if mask is None else mask & in_seq
        zs = [_dot_nt(q_ref[0:n, hs], kbuf[slot, :, hs]) * scale for hs in heads]
        csums = []
        for z in zs:
            sp = jnp.maximum(z, 0.0) + jnp.log(1.0 + jnp.exp(-jnp.abs(z)))
            if mask is not None:
                sp = jnp.where(mask, sp, 0.0)
            csums.append(_dot(sp.astype(BF16), suffix))
        run_min = None
        outs = []
        for h, (z, csum) in enumerate(zip(zs, csums)):
            run = run_ref[h, 0:n, :]
            w = jnp.exp(z - csum - run)
            if mask is not None:
                w = jnp.where(mask, w, 0.0)
            outs.append(_dot(w.astype(BF16), vbuf[slot, :, heads[h]]))
            run = run + csum[:, 0:1]
            run_ref[h, 0:n, :] = run
            run_min = run if run_min is None else jnp.minimum(run_min, run)
        acc_ref[0:n, :] += jnp.concatenate(outs, axis=-1)
        top = jnp.min(run_min[0:SB_TOP])
        rest = jnp.min(run_min[SB_TOP:n]) if n > SB_TOP else None
        return top, rest

    def step(j, top, rest, pad):
        slot = fetch(j)

        def top_rows():
            return visit(j, slot, False, pad, SB_TOP)[0], rest

        def all_rows():
            return visit(j, slot, False, pad, CHUNK)

        return lax.cond(rest >= SB_UNDERFLOW, top_rows, all_rows)

    def first_two_chunks():
        n2 = 2 * CHUNK
        rows = lax.broadcasted_iota(jnp.int32, (CHUNK, n2), 0)
        cols = lax.broadcasted_iota(jnp.int32, (CHUNK, n2), 1)
        key_pos = cols + (i - 1) * CHUNK
        mask = (key_pos < rows + i * CHUNK) & (key_pos >= PAD)
        sr = lax.broadcasted_iota(jnp.int32, (n2, n2), 0)
        sc = lax.broadcasted_iota(jnp.int32, (n2, n2), 1)
        suffix2 = (sr >= sc).astype(BF16)
        s_prev, s_cur = (i - 1) % SB_RING, i % SB_RING
        zs = [jnp.concatenate([_dot_nt(q_ref[:, hs], kbuf[s_prev, :, hs]),
                               _dot_nt(q_ref[:, hs], kbuf[s_cur, :, hs])], axis=1) * scale for hs in heads]
        csums = []
        for z in zs:
            sp = jnp.maximum(z, 0.0) + jnp.log(1.0 + jnp.exp(-jnp.abs(z)))
            csums.append(_dot(jnp.where(mask, sp, 0.0).astype(BF16), suffix2))
        run_min = None
        outs = []
        for h, (z, csum) in enumerate(zip(zs, csums)):
            w = jnp.where(mask, jnp.exp(z - csum), 0.0).astype(BF16)
            outs.append(_dot(w[:, 0:CHUNK], vbuf[s_prev, :, heads[h]])
                        + _dot(w[:, CHUNK:n2], vbuf[s_cur, :, heads[h]]))
            run = csum[:, 0:1]
            run_ref[h] = run
            run_min = run if run_min is None else jnp.minimum(run_min, run)
        acc_ref[...] = jnp.concatenate(outs, axis=-1)
        return jnp.min(run_min[0:SB_TOP]), jnp.min(run_min[SB_TOP:CHUNK])

    top, rest = lax.cond(i == 0, lambda: visit(0, 0, True, True, CHUNK), first_two_chunks)

    def body(carry):
        j, top, rest = carry
        top, rest = step(j, top, rest, False)
        return j - 1, top, rest

    def cond(carry):
        j, top, rest = carry
        return (j >= 1) & (jnp.minimum(top, rest) < SB_UNDERFLOW)

    j, top, rest = lax.while_loop(cond, body, (i - 2, top, rest))

    @pl.when((j == 0) & (jnp.minimum(top, rest) < SB_UNDERFLOW))
    def _():
        step(0, top, rest, True)

    o_ref[...] = acc_ref[...].astype(o_ref.dtype)


def _stick_breaking(proj):
    n_rows = proj.shape[0]

    def col(start):
        assert start % D_MODEL == 0
        return pl.BlockSpec((CHUNK, D_MODEL), lambda i, s=start // D_MODEL: (i, s))

    return pl.pallas_call(
        _sb_kernel,
        grid=(n_rows // CHUNK,),
        in_specs=[col(COL_SB_Q), col(COL_SB_K), col(COL_SB_V), pl.BlockSpec(memory_space=pl.ANY)],
        out_specs=pl.BlockSpec((CHUNK, D_MODEL), lambda i: (i, 0)),
        out_shape=jax.ShapeDtypeStruct((n_rows, D_MODEL), BF16),
        scratch_shapes=[
            pltpu.VMEM((SB_RING + 1, CHUNK, D_MODEL), BF16),
            pltpu.VMEM((SB_RING + 1, CHUNK, D_MODEL), BF16),
            pltpu.VMEM((CHUNK, D_MODEL), F32),
            pltpu.VMEM((SB_HEADS, CHUNK, 1), F32),
            pltpu.SemaphoreType.DMA((2,)),
        ],
        compiler_params=pltpu.CompilerParams(
            dimension_semantics=("arbitrary",), vmem_limit_bytes=VMEM_LIMIT),
    )(proj, proj, proj, proj)


def _merge_kernel(ya_ref, yb_ref, yc_ref, yd_ref, ga_ref, gb_ref, gc_ref, gd_ref, h_ref,
                  wb_ref, wo_ref, nw_ref, o_ref):
    merged = None
    branches = ((ya_ref, ga_ref), (yb_ref, gb_ref), (yc_ref, gc_ref), (yd_ref, gd_ref))
    for n, (y_ref, g_ref) in enumerate(branches):
        up = _dot(y_ref[...], wb_ref[n])
        gate = _sigmoid(g_ref[...].astype(F32))
        merged = gate * up if merged is None else merged + gate * up
    mix = _dot(merged.astype(BF16), wo_ref[...])
    o_ref[...] = h_ref[...] + _rmsnorm(mix, nw_ref[...])


def _merge(ya, yb, yc, yd, proj, h, w_branch, w_out, norm_w, layer):
    n_rows = h.shape[0]
    tm = _row_tile(n_rows, 384)
    row_block = pl.BlockSpec((tm, D_MODEL), lambda i: (i, 0))
    return pl.pallas_call(
        _merge_kernel,
        grid=(n_rows // tm,),
        in_specs=[
            row_block, row_block, row_block, row_block,
        ] + [
            pl.BlockSpec((tm, D_MODEL), lambda i, s=COL_GATE // D_MODEL + n: (i, s)) for n in range(N_BRANCH)
        ] + [
            row_block,
            pl.BlockSpec((None, N_BRANCH, D_MODEL, D_MODEL), lambda i: (layer, 0, 0, 0),
                         pipeline_mode=pl.Buffered(1)),
            pl.BlockSpec((None, D_MODEL, D_MODEL), lambda i: (layer, 0, 0), pipeline_mode=pl.Buffered(1)),
            pl.BlockSpec((1, D_MODEL), lambda i: (0, 0)),
        ],
        out_specs=row_block,
        out_shape=jax.ShapeDtypeStruct((n_rows, D_MODEL), F32),
        compiler_params=pltpu.CompilerParams(
            dimension_semantics=("parallel",), vmem_limit_bytes=VMEM_LIMIT),
    )(ya, yb, yc, yd, proj, proj, proj, proj, h, w_branch, w_out, norm_w)


def _ffn_kernel(*refs, n_pieces):
    h_refs = refs[:n_pieces]
    w1_ref, w2_ref, n1_ref, n2_ref, o_ref = refs[n_pieces:]
    h = h_refs[0][...] if n_pieces == 1 else jnp.concatenate([r[...] for r in h_refs], axis=0)
    xn = _rmsnorm(h, n1_ref[...]).astype(BF16)
    f = _dot(xn, w1_ref[...])
    act = (_silu(f[:, 0:D_FF]) * f[:, D_FF:2 * D_FF]).astype(BF16)
    o_ref[...] = h + _rmsnorm(_dot(act, w2_ref[...]), n2_ref[...])


def _ffn(h, w1, w2, n1, n2, layer, drop_first_chunk):
    n_rows = h.shape[0]
    if drop_first_chunk:
        n_out = n_rows - CHUNK
        tm = max(t for t in range(CHUNK, FFN_FINAL_TILE + 1, CHUNK) if n_out % t == 0)
        n_pieces = tm // CHUNK
        h_specs = [pl.BlockSpec((CHUNK, D_MODEL), lambda i, k=k: (i * n_pieces + 1 + k, 0))
                   for k in range(n_pieces)]
    else:
        n_out = n_rows
        tm = _row_tile(n_rows, 384)
        n_pieces = 1
        h_specs = [pl.BlockSpec((tm, D_MODEL), lambda i: (i, 0))]
    resident = pl.Buffered(1)
    return pl.pallas_call(
        functools.partial(_ffn_kernel, n_pieces=n_pieces),
        grid=(n_out // tm,),
        in_specs=h_specs + [
            pl.BlockSpec((None, D_MODEL, 2 * D_FF), lambda i: (layer, 0, 0), pipeline_mode=resident),
            pl.BlockSpec((None, D_FF, D_MODEL), lambda i: (layer, 0, 0), pipeline_mode=resident),
            pl.BlockSpec((1, D_MODEL), lambda i: (0, 0)),
            pl.BlockSpec((1, D_MODEL), lambda i: (0, 0)),
        ],
        out_specs=pl.BlockSpec((tm, D_MODEL), lambda i: (i, 0)),
        out_shape=jax.ShapeDtypeStruct((n_out, D_MODEL), F32),
        compiler_params=pltpu.CompilerParams(
            dimension_semantics=("parallel",), vmem_limit_bytes=VMEM_LIMIT),
    )(*([h] * n_pieces), w1, w2, n1, n2)


def _rope_tables(n_chunks):
    half = RET_QK_DIM // 2
    inv = ROPE_BASE ** (-np.arange(half, dtype=np.float64) / half)
    chunk_ang = (np.arange(n_chunks, dtype=np.float64) * CHUNK)[:, None, None] * inv[None, None, :]
    base_ang = np.arange(CHUNK, dtype=np.float64)[:, None] * inv[None, :]
    return tuple(jnp.asarray(t, F32) for t in
                 (np.cos(chunk_ang), np.sin(chunk_ang), np.cos(base_ang), np.sin(base_ang)))


def _head_expand_matrix():
    e = np.zeros((DT_LANES, SSD_INNER), np.float32)
    for h in range(SSD_HEADS):
        e[h, h * SSD_HEAD_DIM:(h + 1) * SSD_HEAD_DIM] = 1.0
    return jnp.asarray(e, BF16)


def _pad_lanes(v):
    return jnp.pad(v.astype(F32), (0, DT_LANES - v.shape[0]))[None, :]


def kernel(x, meta, w_in, conv_a, ssd_conv_w, ssd_conv_b, ssd_dt_bias, ssd_a_log, ssd_d, ssd_norm,
           w_branch, w_out, w_ffn_in, w_ffn_out, norm_mix_pre, norm_mix_post, norm_ffn_pre,
           norm_ffn_post):
    batch, seq, _ = x.shape
    assert batch == 1 and seq % CHUNK == 0
    depth = w_in.shape[0]
    n_rows = seq + CHUNK
    h = jnp.concatenate([jnp.zeros((PAD, D_MODEL), F32), meta.astype(F32), x[0].astype(F32)], axis=0)
    rope = _rope_tables(n_rows // CHUNK)
    expand = _head_expand_matrix()
    dt_end = DT_COL_START + SSD_HEADS
    w_main = jnp.concatenate([w_in[:, :, :DT_COL_START], w_in[:, :, dt_end:]], axis=2).astype(BF16)
    w_dt = jnp.pad(w_in[:, :, DT_COL_START:dt_end], ((0, 0), (0, 0), (0, DT_LANES - SSD_HEADS))).astype(BF16)
    w_branch_b, w_out_b = w_branch.astype(BF16), w_out.astype(BF16)
    w_ffn_in_b, w_ffn_out_b = w_ffn_in.astype(BF16), w_ffn_out.astype(BF16)
    for l in range(depth):
        proj, dt_raw = _inproj(h, norm_mix_pre[l][None, :], w_main, w_dt, l)
        ya, yb, yc = _mixers(
            proj, dt_raw, rope, conv_a[l], ssd_conv_w[l], ssd_conv_b[l][None, :],
            _pad_lanes(ssd_dt_bias[l]), _pad_lanes(ssd_a_log[l]),
            jnp.repeat(ssd_d[l].astype(F32), SSD_HEAD_DIM)[None, :], ssd_norm[l][None, :], expand)
        yd = _stick_breaking(proj)
        h = _merge(ya, yb, yc, yd, proj, h, w_branch_b, w_out_b, norm_mix_post[l][None, :], l)
        h = _ffn(h, w_ffn_in_b, w_ffn_out_b, norm_ffn_pre[l][None, :], norm_ffn_post[l][None, :], l,
                 drop_first_chunk=(l == depth - 1))
    return h[None].astype(x.dtype)
```

```python
import functools
import math

import numpy as np
import jax
import jax.numpy as jnp
from jax import lax
from jax.experimental import pallas as pl
from jax.experimental.pallas import tpu as pltpu

F32 = jnp.float32
BF16 = jnp.bfloat16

D_MODEL = 1024
N_META = 16
CHUNK = 128
PAD = CHUNK - N_META
EPS = 1e-6

SSD_HEAD_DIM = 64
SSD_HEADS = 16
SSD_INNER = 1024
SSD_GROUPS = 4
SSD_STATE = 128
SSD_CONV_K = 4
SSD_CONV_DIM = SSD_INNER + 2 * SSD_GROUPS * SSD_STATE
SSD_GROUP_WIDTH = SSD_INNER // SSD_GROUPS
CONV_A_K = 3
RET_HEADS = 4
RET_QK_DIM = 256
RET_V_DIM = 256
ROPE_BASE = 10000.0
SB_HEADS = 8
SB_HEAD_DIM = 128
N_BRANCH = 4
D_FF = 2816

COL_CONV = 0
COL_SSD_Z = 3072
COL_SSD_XBC = 4096
COL_RET_QK = 6144
COL_RET_VG = 8192
COL_SB_Q = 10240
COL_SB_K = 11264
COL_SB_V = 12288
COL_GATE = 13312
PROJ_WIDTH = 17408
DT_COL_START = 6144
DT_LANES = 128
CONV_STRIP = 256
FFN_FINAL_TILE = 512

HIST = 8
SB_UNDERFLOW = 104.0
SB_RING = 4
SB_TOP = 32
VMEM_LIMIT = 56 * 1024 * 1024


def _rmsnorm(x, w):
    return x * lax.rsqrt(jnp.mean(x * x, axis=-1, keepdims=True) + EPS) * w


def _sigmoid(x):
    return 1.0 / (1.0 + jnp.exp(-x))


def _silu(x):
    return x * _sigmoid(x)


def _softplus(x):
    return jnp.maximum(x, 0.0) + jnp.log1p(jnp.exp(-jnp.abs(x)))


def _split3(x):
    hi = x.astype(BF16)
    r = x - hi.astype(F32)
    mid = r.astype(BF16)
    lo = (r - mid.astype(F32)).astype(BF16)
    return hi, mid, lo


def _dot(a, b):
    return jnp.dot(a, b, preferred_element_type=F32)


def _dot_nt(a, b):
    return lax.dot_general(a, b, (((1,), (1,)), ((), ())), preferred_element_type=F32)


def _dot_tn(a, b):
    return lax.dot_general(a, b, (((0,), (0,)), ((), ())), preferred_element_type=F32)


def _row_tile(n_rows, target):
    best = 16
    for t in range(16, min(n_rows, target) + 1, 16):
        if n_rows % t == 0:
            best = t
    return best


def _inproj_kernel(x_ref, nw_ref, w_ref, wdt_ref, o_ref, dt_ref, xn_ref):
    @pl.when(pl.program_id(1) == 0)
    def _():
        xn = _rmsnorm(x_ref[...], nw_ref[...]).astype(BF16)
        xn_ref[...] = xn
        dt_ref[...] = _dot(xn, wdt_ref[...])

    o_ref[...] = _dot(xn_ref[...], w_ref[...]).astype(o_ref.dtype)


def _inproj(h, norm_w, w_main, w_dt, layer):
    n_rows = h.shape[0]
    tm = _row_tile(n_rows, 1376)
    tn = 1024
    return pl.pallas_call(
        _inproj_kernel,
        grid=(n_rows // tm, PROJ_WIDTH // tn),
        in_specs=[
            pl.BlockSpec((tm, D_MODEL), lambda i, j: (i, 0)),
            pl.BlockSpec((1, D_MODEL), lambda i, j: (0, 0)),
            pl.BlockSpec((None, D_MODEL, tn), lambda i, j: (layer, 0, j)),
            pl.BlockSpec((None, D_MODEL, DT_LANES), lambda i, j: (layer, 0, 0)),
        ],
        out_specs=[
            pl.BlockSpec((tm, tn), lambda i, j: (i, j)),
            pl.BlockSpec((tm, DT_LANES), lambda i, j: (i, 0)),
        ],
        out_shape=[
            jax.ShapeDtypeStruct((n_rows, PROJ_WIDTH), BF16),
            jax.ShapeDtypeStruct((n_rows, DT_LANES), F32),
        ],
        scratch_shapes=[pltpu.VMEM((tm, D_MODEL), BF16)],
        compiler_params=pltpu.CompilerParams(
            dimension_semantics=("parallel", "arbitrary"), vmem_limit_bytes=VMEM_LIMIT),
    )(h, norm_w, w_main, w_dt)


def _mixer_kernel(conv_ref, z_ref, xbc_ref, qk_ref, vg_ref, dt_ref, ccos_ref, csin_ref,
                  bcos_ref, bsin_ref,
                  conva_ref, sconvw_ref, sconvb_ref, dtb_ref, alog_ref, dskip_ref, snorm_ref,
                  expand_ref,
                  ya_ref, yb_ref, yc_ref,
                  uhist_ref, xhist_ref, xs_ref, bc_ref, sstate_ref, rstate_ref, dmask_ref, kdec_ref,
                  qdec_ref):
    c = pl.program_id(0)
    li = lax.broadcasted_iota(jnp.int32, (CHUNK, CHUNK), 0)
    si = lax.broadcasted_iota(jnp.int32, (CHUNK, CHUNK), 1)
    causal = li >= si

    @pl.when(c == 0)
    def _():
        uhist_ref[0:HIST, :] = jnp.zeros((HIST, D_MODEL), F32)
        xhist_ref[0:HIST, :] = jnp.zeros((HIST, SSD_CONV_DIM), F32)
        sstate_ref[...] = jnp.zeros_like(sstate_ref)
        rstate_ref[...] = jnp.zeros_like(rstate_ref)
        rel = (li - si).astype(F32)
        row_w = lax.broadcasted_iota(jnp.int32, (CHUNK, RET_QK_DIM), 0).astype(F32)
        for h in range(RET_HEADS):
            log_gamma = math.log(1.0 - 2.0 ** (-5.0 - h))
            dmask_ref[h] = jnp.where(causal, jnp.exp(log_gamma * jnp.maximum(rel, 0.0)), 0.0)
            kdec_ref[h] = jnp.exp(log_gamma * (CHUNK - 1.0 - row_w))
            qdec_ref[h] = jnp.exp(log_gamma * (row_w + 1.0))

    def valid(shape):
        return (lax.broadcasted_iota(jnp.int32, shape, 0) + c * CHUNK) >= PAD

    strip_valid = valid((CHUNK, CONV_STRIP))

    def causal_conv(u, hist_ref, cs, taps_ref):
        k_taps = taps_ref.shape[0]
        hist_ref[HIST:HIST + CHUNK, cs] = u
        out = u * taps_ref[k_taps - 1:k_taps, cs]
        for t in range(k_taps - 1):
            off = HIST - (k_taps - 1) + t
            out = out + hist_ref[off:off + CHUNK, cs] * taps_ref[t:t + 1, cs]
        hist_ref[0:HIST, cs] = hist_ref[CHUNK:CHUNK + HIST, cs]
        return out

    for n in range(D_MODEL // CONV_STRIP):
        cs = slice(n * CONV_STRIP, (n + 1) * CONV_STRIP)
        b_gate = conv_ref[:, cs].astype(F32)
        c_gate = conv_ref[:, D_MODEL + cs.start:D_MODEL + cs.stop].astype(F32)
        xa = conv_ref[:, 2 * D_MODEL + cs.start:2 * D_MODEL + cs.stop].astype(F32)
        u = jnp.where(strip_valid, c_gate * xa, 0.0)
        ya_ref[:, cs] = (b_gate * causal_conv(u, uhist_ref, cs, conva_ref)).astype(ya_ref.dtype)

    for n in range(SSD_CONV_DIM // CONV_STRIP):
        cs = slice(n * CONV_STRIP, (n + 1) * CONV_STRIP)
        xin = jnp.where(strip_valid, xbc_ref[:, cs].astype(F32), 0.0)
        xc = _silu(causal_conv(xin, xhist_ref, cs, sconvw_ref) + sconvb_ref[:, cs])
        if cs.stop <= SSD_INNER:
            xs_ref[:, cs] = jnp.where(strip_valid, xc, 0.0)
        else:
            bc_ref[:, cs.start - SSD_INNER:cs.stop - SSD_INNER] = xc.astype(BF16)
    xs = xs_ref[...]
    bm = bc_ref[:, 0:SSD_GROUPS * SSD_STATE]
    cm = bc_ref[:, SSD_GROUPS * SSD_STATE:2 * SSD_GROUPS * SSD_STATE]

    dtv = _softplus(dt_ref[...] + dtb_ref[...])
    a = -jnp.exp(alog_ref[...]) * dtv
    tri = causal.astype(BF16)
    acs = sum(_dot(tri, part) for part in _split3(a))
    acs_t = acs.T
    acs_last = acs[CHUNK - 1:CHUNK, :]
    expand = expand_ref[...]

    def expand_heads(v):
        return sum(_dot(part, expand) for part in _split3(v))

    dt_x = expand_heads(dtv)
    eacs_x = jnp.exp(expand_heads(acs))
    dstate_x = jnp.exp(expand_heads(acs_last - acs))
    cdecay_x = eacs_x[CHUNK - 1:CHUNK, :]
    xdt = xs * dt_x
    xdt_b = xdt.astype(BF16)
    xw_b = (xdt * dstate_x).astype(BF16)

    hpg = SSD_HEADS // SSD_GROUPS
    groups = [slice(g * SSD_GROUP_WIDTH, (g + 1) * SSD_GROUP_WIDTH) for g in range(SSD_GROUPS)]
    ccs = [cm[:, g * SSD_STATE:(g + 1) * SSD_STATE] for g in range(SSD_GROUPS)]
    bcs = [bm[:, g * SSD_STATE:(g + 1) * SSD_STATE] for g in range(SSD_GROUPS)]
    cbs = [_dot_nt(cc, bc) for cc, bc in zip(ccs, bcs)]
    prevs = [sstate_ref[g] for g in range(SSD_GROUPS)]
    y_offs = [_dot(cc, prev.astype(BF16)) for cc, prev in zip(ccs, prevs)]
    new_states = [_dot_tn(bc, xw_b[:, gs]) for bc, gs in zip(bcs, groups)]
    for g in range(SSD_GROUPS):
        sstate_ref[g] = prevs[g] * cdecay_x[:, groups[g]] + new_states[g]
    yd = []
    for h in range(SSD_HEADS):
        diff = acs[:, h:h + 1] - acs_t[h:h + 1, :]
        seg = jnp.exp(jnp.where(causal, diff, -jnp.inf))
        hs = slice(h * SSD_HEAD_DIM, (h + 1) * SSD_HEAD_DIM)
        yd.append(_dot((cbs[h // hpg] * seg).astype(BF16), xdt_b[:, hs]))
    y = jnp.concatenate(yd, axis=-1) + jnp.concatenate(y_offs, axis=-1) * eacs_x + xs * dskip_ref[...]
    y = y * _silu(z_ref[...].astype(F32))
    yn = []
    for g in range(SSD_GROUPS):
        yg = y[:, g * SSD_GROUP_WIDTH:(g + 1) * SSD_GROUP_WIDTH]
        yn.append(yg * lax.rsqrt(jnp.mean(yg * yg, axis=-1, keepdims=True) + EPS))
    yb_ref[...] = (jnp.concatenate(yn, axis=-1) * snorm_ref[...]).astype(yb_ref.dtype)

    cc_, sc_ = ccos_ref[0], csin_ref[0]
    bcos, bsin = bcos_ref[...], bsin_ref[...]
    cos = cc_ * bcos - sc_ * bsin
    sin = sc_ * bcos + cc_ * bsin
    half = RET_QK_DIM // 2
    ko = RET_HEADS * RET_QK_DIM
    go = RET_HEADS * RET_V_DIM
    v_valid = valid((CHUNK, RET_V_DIM))

    def rotated(off):
        x1 = qk_ref[:, off:off + half].astype(F32)
        x2 = qk_ref[:, off + half:off + RET_QK_DIM].astype(F32)
        return jnp.concatenate([x1 * cos - x2 * sin, x1 * sin + x2 * cos], axis=-1)

    qrs, krs, vrs, prevs = [], [], [], []
    for h in range(RET_HEADS):
        qrs.append(rotated(h * RET_QK_DIM).astype(BF16))
        krs.append(rotated(ko + h * RET_QK_DIM) * (RET_QK_DIM ** -0.5))
        vrs.append(jnp.where(v_valid, vg_ref[:, h * RET_V_DIM:(h + 1) * RET_V_DIM].astype(F32), 0.0).astype(BF16))
        prevs.append(rstate_ref[h])
    scores = [_dot_nt(qr, kr.astype(BF16)) for qr, kr in zip(qrs, krs)]
    y_crs = [_dot(qr, prev.astype(BF16)) for qr, prev in zip(qrs, prevs)]
    kvs = [_dot_tn((krs[h] * kdec_ref[h]).astype(BF16), vrs[h]) for h in range(RET_HEADS)]
    for h in range(RET_HEADS):
        log_gamma = math.log(1.0 - 2.0 ** (-5.0 - h))
        rstate_ref[h] = prevs[h] * math.exp(log_gamma * CHUNK) + kvs[h]
    y_ins = [_dot((scores[h] * dmask_ref[h]).astype(BF16), vrs[h]) for h in range(RET_HEADS)]
    yc = []
    for h in range(RET_HEADS):
        gate = vg_ref[:, go + h * RET_V_DIM:go + (h + 1) * RET_V_DIM].astype(F32)
        yh = y_ins[h] + y_crs[h] * qdec_ref[h]
        mu = jnp.mean(yh, axis=-1, keepdims=True)
        yh = yh - mu
        var = jnp.mean(yh * yh, axis=-1, keepdims=True)
        yc.append(yh * lax.rsqrt(var + EPS) * _silu(gate))
    yc_ref[...] = jnp.concatenate(yc, axis=-1).astype(yc_ref.dtype)


def _mixers(proj, dt_raw, rope, conv_a, sconv_w, sconv_b, dt_bias, a_log, d_skip, snorm, expand):
    n_rows = proj.shape[0]
    nc = n_rows // CHUNK
    chunk_cos, chunk_sin, base_cos, base_sin = rope

    def col(width, start):
        assert start % width == 0
        return pl.BlockSpec((CHUNK, width), lambda c, s=start // width: (c, s))

    def full(arr):
        return pl.BlockSpec(arr.shape, lambda c: (0,) * arr.ndim)

    row_block = pl.BlockSpec((CHUNK, D_MODEL), lambda c: (c, 0))
    chunk_row = pl.BlockSpec((1, 1, RET_QK_DIM // 2), lambda c: (c, 0, 0))
    weights = (base_cos, base_sin, conv_a, sconv_w, sconv_b, dt_bias, a_log, d_skip, snorm, expand)
    return pl.pallas_call(
        _mixer_kernel,
        grid=(nc,),
        in_specs=[
            col(3 * D_MODEL, COL_CONV), col(D_MODEL, COL_SSD_Z), col(SSD_CONV_DIM, COL_SSD_XBC),
            col(2048, COL_RET_QK), col(2048, COL_RET_VG),
            pl.BlockSpec((CHUNK, DT_LANES), lambda c: (c, 0)), chunk_row, chunk_row,
        ] + [full(w) for w in weights],
        out_specs=[row_block, row_block, row_block],
        out_shape=[jax.ShapeDtypeStruct((n_rows, D_MODEL), BF16)] * 3,
        scratch_shapes=[
            pltpu.VMEM((HIST + CHUNK, D_MODEL), F32),
            pltpu.VMEM((HIST + CHUNK, SSD_CONV_DIM), F32),
            pltpu.VMEM((CHUNK, SSD_INNER), F32),
            pltpu.VMEM((CHUNK, 2 * SSD_GROUPS * SSD_STATE), BF16),
            pltpu.VMEM((SSD_GROUPS, SSD_STATE, SSD_GROUP_WIDTH), F32),
            pltpu.VMEM((RET_HEADS, RET_QK_DIM, RET_V_DIM), F32),
            pltpu.VMEM((RET_HEADS, CHUNK, CHUNK), F32),
            pltpu.VMEM((RET_HEADS, CHUNK, RET_QK_DIM), F32),
            pltpu.VMEM((RET_HEADS, CHUNK, RET_QK_DIM), F32),
        ],
        compiler_params=pltpu.CompilerParams(
            dimension_semantics=("arbitrary",), vmem_limit_bytes=VMEM_LIMIT),
    )(proj, proj, proj, proj, proj, dt_raw, chunk_cos, chunk_sin, *weights)


def _sb_kernel(q_ref, k_ref, v_ref, proj_hbm, o_ref, kbuf, vbuf, acc_ref, run_ref, sem):
    i = pl.program_id(0)
    kbuf[i % SB_RING] = k_ref[...]
    vbuf[i % SB_RING] = v_ref[...]
    acc_ref[...] = jnp.zeros_like(acc_ref)
    run_ref[...] = jnp.zeros_like(run_ref)
    scale = SB_HEAD_DIM ** -0.5
    heads = [slice(h * SB_HEAD_DIM, (h + 1) * SB_HEAD_DIM) for h in range(SB_HEADS)]
    sfx_r = lax.broadcasted_iota(jnp.int32, (CHUNK, CHUNK), 0)
    sfx_c = lax.broadcasted_iota(jnp.int32, (CHUNK, CHUNK), 1)
    suffix = (sfx_r >= sfx_c).astype(BF16)

    def fetch(j):
        far = (i - j) >= SB_RING

        @pl.when(far)
        def _():
            rows_j = pl.ds(pl.multiple_of(j * CHUNK, CHUNK), CHUNK)
            copies = [
                pltpu.make_async_copy(proj_hbm.at[rows_j, pl.ds(COL_SB_K, D_MODEL)], kbuf.at[SB_RING], sem.at[0]),
                pltpu.make_async_copy(proj_hbm.at[rows_j, pl.ds(COL_SB_V, D_MODEL)], vbuf.at[SB_RING], sem.at[1]),
            ]
            for cp in copies:
                cp.start()
            for cp in copies:
                cp.wait()

        return jnp.where(far, SB_RING, j % SB_RING)

    def visit(j, slot, causal, pad, n):
        rows = lax.broadcasted_iota(jnp.int32, (n, CHUNK), 0)
        cols = lax.broadcasted_iota(jnp.int32, (n, CHUNK), 1)
        mask = None
        if causal:
            mask = cols < rows
        if pad:
            in_seq = (cols + j * CHUNK) >= PAD
            mask = in_seq if mask is None else mask & in_seq
        zs = [_dot_nt(q_ref[0:n, hs], kbuf[slot, :, hs]) * scale for hs in heads]
        csums = []
        for z in zs:
            sp = jnp.maximum(z, 0.0) + jnp.log(1.0 + jnp.exp(-jnp.abs(z)))
            if mask is not None:
                sp = jnp.where(mask, sp, 0.0)
            csums.append(_dot(sp.astype(BF16), suffix))
        run_min = None
        outs = []
        for h, (z, csum) in enumerate(zip(zs, csums)):
            run = run_ref[h, 0:n, :]
            w = jnp.exp(z - csum - run)
            if mask is not None:
                w = jnp.where(mask, w, 0.0)
            outs.append(_dot(w.astype(BF16), vbuf[slot, :, heads[h]]))
            run = run + csum[:, 0:1]
            run_ref[h, 0:n, :] = run
            run_min = run if run_min is None else jnp.minimum(run_min, run)
        acc_ref[0:n, :] += jnp.concatenate(outs, axis=-1)
        top = jnp.min(run_min[0:SB_TOP])
        rest = jnp.min(run_min[SB_TOP:n]) if n > SB_TOP else None
        return top, rest

    def step(j, top, rest, pad):
        slot = fetch(j)

        def top_rows():
            return visit(j, slot, False, pad, SB_TOP)[0], rest

        def all_rows():
            return visit(j, slot, False, pad, CHUNK)

        return lax.cond(rest >= SB_UNDERFLOW, top_rows, all_rows)

    def first_two_chunks():
        n2 = 2 * CHUNK
        rows = lax.broadcasted_iota(jnp.int32, (CHUNK, n2), 0)
        cols = lax.broadcasted_iota(jnp.int32, (CHUNK, n2), 1)
        key_pos = cols + (i - 1) * CHUNK
        mask = (key_pos < rows + i * CHUNK) & (key_pos >= PAD)
        sr = lax.broadcasted_iota(jnp.int32, (n2, n2), 0)
        sc = lax.broadcasted_iota(jnp.int32, (n2, n2), 1)
        suffix2 = (sr >= sc).astype(BF16)
        s_prev, s_cur = (i - 1) % SB_RING, i % SB_RING
        zs = [jnp.concatenate([_dot_nt(q_ref[:, hs], kbuf[s_prev, :, hs]),
                               _dot_nt(q_ref[:, hs], kbuf[s_cur, :, hs])], axis=1) * scale for hs in heads]
        csums = []
        for z in zs:
            sp = jnp.maximum(z, 0.0) + jnp.log(1.0 + jnp.exp(-jnp.abs(z)))
            csums.append(_dot(jnp.where(mask, sp, 0.0).astype(BF16), suffix2))
        run_min = None
        outs = []
        for h, (z, csum) in enumerate(zip(zs, csums)):
            w = jnp.where(mask, jnp.exp(z - csum), 0.0).astype(BF16)
            outs.append(_dot(w[:, 0:CHUNK], vbuf[s_prev, :, heads[h]])
                        + _dot(w[:, CHUNK:n2], vbuf[s_cur, :, heads[h]]))
            run = csum[:, 0:1]
            run_ref[h] = run
            run_min = run if run_min is None else jnp.minimum(run_min, run)
        acc_ref[...] = jnp.concatenate(outs, axis=-1)
        return jnp.min(run_min[0:SB_TOP]), jnp.min(run_min[SB_TOP:CHUNK])

    top, rest = lax.cond(i == 0, lambda: visit(0, 0, True, True, CHUNK), first_two_chunks)

    def body(carry):
        j, top, rest = carry
        top, rest = step(j, top, rest, False)
        return j - 1, top, rest

    def cond(carry):
        j, top, rest = carry
        return (j >= 1) & (jnp.minimum(top, rest) < SB_UNDERFLOW)

    j, top, rest = lax.while_loop(cond, body, (i - 2, top, rest))

    @pl.when((j == 0) & (jnp.minimum(top, rest) < SB_UNDERFLOW))
    def _():
        step(0, top, rest, True)

    o_ref[...] = acc_ref[...].astype(o_ref.dtype)


def _stick_breaking(proj):
    n_rows = proj.shape[0]

    def col(start):
        assert start % D_MODEL == 0
        return pl.BlockSpec((CHUNK, D_MODEL), lambda i, s=start // D_MODEL: (i, s))

    return pl.pallas_call(
        _sb_kernel,
        grid=(n_rows // CHUNK,),
        in_specs=[col(COL_SB_Q), col(COL_SB_K), col(COL_SB_V), pl.BlockSpec(memory_space=pl.ANY)],
        out_specs=pl.BlockSpec((CHUNK, D_MODEL), lambda i: (i, 0)),
        out_shape=jax.ShapeDtypeStruct((n_rows, D_MODEL), BF16),
        scratch_shapes=[
            pltpu.VMEM((SB_RING + 1, CHUNK, D_MODEL), BF16),
            pltpu.VMEM((SB_RING + 1, CHUNK, D_MODEL), BF16),
            pltpu.VMEM((CHUNK, D_MODEL), F32),
            pltpu.VMEM((SB_HEADS, CHUNK, 1), F32),
            pltpu.SemaphoreType.DMA((2,)),
        ],
        compiler_params=pltpu.CompilerParams(
            dimension_semantics=("arbitrary",), vmem_limit_bytes=VMEM_LIMIT),
    )(proj, proj, proj, proj)


N_MIXER_IN, N_MIXER_SCRATCH, N_SB_IN = 18, 9, 4


def _mixers_sb_kernel(*refs):
    m_in = refs[:N_MIXER_IN]
    s_in = refs[N_MIXER_IN:N_MIXER_IN + N_SB_IN]
    outs = refs[N_MIXER_IN + N_SB_IN:N_MIXER_IN + N_SB_IN + 4]
    scratch = refs[N_MIXER_IN + N_SB_IN + 4:]
    _mixer_kernel(*m_in, *outs[:3], *scratch[:N_MIXER_SCRATCH])
    _sb_kernel(*s_in, outs[3], *scratch[N_MIXER_SCRATCH:])


def _mixers_sb(proj, dt_raw, rope, conv_a, sconv_w, sconv_b, dt_bias, a_log, d_skip, snorm, expand):
    n_rows = proj.shape[0]
    chunk_cos, chunk_sin, base_cos, base_sin = rope

    def col(width, start):
        assert start % width == 0
        return pl.BlockSpec((CHUNK, width), lambda c, s=start // width: (c, s))

    def full(arr):
        return pl.BlockSpec(arr.shape, lambda c: (0,) * arr.ndim)

    row_block = pl.BlockSpec((CHUNK, D_MODEL), lambda c: (c, 0))
    chunk_row = pl.BlockSpec((1, 1, RET_QK_DIM // 2), lambda c: (c, 0, 0))
    weights = (base_cos, base_sin, conv_a, sconv_w, sconv_b, dt_bias, a_log, d_skip, snorm, expand)
    mixer_specs = [
        col(3 * D_MODEL, COL_CONV), col(D_MODEL, COL_SSD_Z), col(SSD_CONV_DIM, COL_SSD_XBC),
        col(2048, COL_RET_QK), col(2048, COL_RET_VG),
        pl.BlockSpec((CHUNK, DT_LANES), lambda c: (c, 0)), chunk_row, chunk_row,
    ] + [full(w) for w in weights]
    sb_specs = [col(D_MODEL, COL_SB_Q), col(D_MODEL, COL_SB_K), col(D_MODEL, COL_SB_V),
                pl.BlockSpec(memory_space=pl.ANY)]
    assert len(mixer_specs) == N_MIXER_IN and len(sb_specs) == N_SB_IN
    return pl.pallas_call(
        _mixers_sb_kernel,
        grid=(n_rows // CHUNK,),
        in_specs=mixer_specs + sb_specs,
        out_specs=[row_block] * 4,
        out_shape=[jax.ShapeDtypeStruct((n_rows, D_MODEL), BF16)] * 4,
        scratch_shapes=[
            pltpu.VMEM((HIST + CHUNK, D_MODEL), F32),
            pltpu.VMEM((HIST + CHUNK, SSD_CONV_DIM), F32),
            pltpu.VMEM((CHUNK, SSD_INNER), F32),
            pltpu.VMEM((CHUNK, 2 * SSD_GROUPS * SSD_STATE), BF16),
            pltpu.VMEM((SSD_GROUPS, SSD_STATE, SSD_GROUP_WIDTH), F32),
            pltpu.VMEM((RET_HEADS, RET_QK_DIM, RET_V_DIM), F32),
            pltpu.VMEM((RET_HEADS, CHUNK, CHUNK), F32),
            pltpu.VMEM((RET_HEADS, CHUNK, RET_QK_DIM), F32),
            pltpu.VMEM((RET_HEADS, CHUNK, RET_QK_DIM), F32),
            pltpu.VMEM((SB_RING + 1, CHUNK, D_MODEL), BF16),
            pltpu.VMEM((SB_RING + 1, CHUNK, D_MODEL), BF16),
            pltpu.VMEM((CHUNK, D_MODEL), F32),
            pltpu.VMEM((SB_HEADS, CHUNK, 1), F32),
            pltpu.SemaphoreType.DMA((2,)),
        ],
        compiler_params=pltpu.CompilerParams(
            dimension_semantics=("arbitrary",), vmem_limit_bytes=VMEM_LIMIT),
    )(proj, proj, proj, proj, proj, dt_raw, chunk_cos, chunk_sin, *weights, proj, proj, proj, proj)


def _merge_kernel(ya_ref, yb_ref, yc_ref, yd_ref, ga_ref, gb_ref, gc_ref, gd_ref, h_ref,
                  wb_ref, wo_ref, nw_ref, o_ref):
    merged = None
    branches = ((ya_ref, ga_ref), (yb_ref, gb_ref), (yc_ref, gc_ref), (yd_ref, gd_ref))
    for n, (y_ref, g_ref) in enumerate(branches):
        up = _dot(y_ref[...], wb_ref[n])
        gate = _sigmoid(g_ref[...].astype(F32))
        merged = gate * up if merged is None else merged + gate * up
    mix = _dot(merged.astype(BF16), wo_ref[...])
    o_ref[...] = h_ref[...] + _rmsnorm(mix, nw_ref[...])


def _merge(ya, yb, yc, yd, proj, h, w_branch, w_out, norm_w, layer):
    n_rows = h.shape[0]
    tm = _row_tile(n_rows, 384)
    row_block = pl.BlockSpec((tm, D_MODEL), lambda i: (i, 0))
    return pl.pallas_call(
        _merge_kernel,
        grid=(n_rows // tm,),
        in_specs=[
            row_block, row_block, row_block, row_block,
        ] + [
            pl.BlockSpec((tm, D_MODEL), lambda i, s=COL_GATE // D_MODEL + n: (i, s)) for n in range(N_BRANCH)
        ] + [
            row_block,
            pl.BlockSpec((None, N_BRANCH, D_MODEL, D_MODEL), lambda i: (layer, 0, 0, 0),
                         pipeline_mode=pl.Buffered(1)),
            pl.BlockSpec((None, D_MODEL, D_MODEL), lambda i: (layer, 0, 0), pipeline_mode=pl.Buffered(1)),
            pl.BlockSpec((1, D_MODEL), lambda i: (0, 0)),
        ],
        out_specs=row_block,
        out_shape=jax.ShapeDtypeStruct((n_rows, D_MODEL), F32),
        compiler_params=pltpu.CompilerParams(
            dimension_semantics=("parallel",), vmem_limit_bytes=VMEM_LIMIT),
    )(ya, yb, yc, yd, proj, proj, proj, proj, h, w_branch, w_out, norm_w)


def _ffn_kernel(*refs, n_pieces):
    h_refs = refs[:n_pieces]
    w1_ref, w2_ref, n1_ref, n2_ref, o_ref = refs[n_pieces:]
    h = h_refs[0][...] if n_pieces == 1 else jnp.concatenate([r[...] for r in h_refs], axis=0)
    xn = _rmsnorm(h, n1_ref[...]).astype(BF16)
    f = _dot(xn, w1_ref[...])
    act = (_silu(f[:, 0:D_FF]) * f[:, D_FF:2 * D_FF]).astype(BF16)
    o_ref[...] = h + _rmsnorm(_dot(act, w2_ref[...]), n2_ref[...])


def _ffn(h, w1, w2, n1, n2, layer, drop_first_chunk):
    n_rows = h.shape[0]
    if drop_first_chunk:
        n_out = n_rows - CHUNK
        tm = max(t for t in range(CHUNK, FFN_FINAL_TILE + 1, CHUNK) if n_out % t == 0)
        n_pieces = tm // CHUNK
        h_specs = [pl.BlockSpec((CHUNK, D_MODEL), lambda i, k=k: (i * n_pieces + 1 + k, 0))
                   for k in range(n_pieces)]
    else:
        n_out = n_rows
        tm = _row_tile(n_rows, 384)
        n_pieces = 1
        h_specs = [pl.BlockSpec((tm, D_MODEL), lambda i: (i, 0))]
    resident = pl.Buffered(1)
    return pl.pallas_call(
        functools.partial(_ffn_kernel, n_pieces=n_pieces),
        grid=(n_out // tm,),
        in_specs=h_specs + [
            pl.BlockSpec((None, D_MODEL, 2 * D_FF), lambda i: (layer, 0, 0), pipeline_mode=resident),
            pl.BlockSpec((None, D_FF, D_MODEL), lambda i: (layer, 0, 0), pipeline_mode=resident),
            pl.BlockSpec((1, D_MODEL), lambda i: (0, 0)),
            pl.BlockSpec((1, D_MODEL), lambda i: (0, 0)),
        ],
        out_specs=pl.BlockSpec((tm, D_MODEL), lambda i: (i, 0)),
        out_shape=jax.ShapeDtypeStruct((n_out, D_MODEL), F32),
        compiler_params=pltpu.CompilerParams(
            dimension_semantics=("parallel",), vmem_limit_bytes=VMEM_LIMIT),
    )(*([h] * n_pieces), w1, w2, n1, n2)


def _rope_tables(n_chunks):
    half = RET_QK_DIM // 2
    inv = ROPE_BASE ** (-np.arange(half, dtype=np.float64) / half)
    chunk_ang = (np.arange(n_chunks, dtype=np.float64) * CHUNK)[:, None, None] * inv[None, None, :]
    base_ang = np.arange(CHUNK, dtype=np.float64)[:, None] * inv[None, :]
    return tuple(jnp.asarray(t, F32) for t in
                 (np.cos(chunk_ang), np.sin(chunk_ang), np.cos(base_ang), np.sin(base_ang)))


def _head_expand_matrix():
    e = np.zeros((DT_LANES, SSD_INNER), np.float32)
    for h in range(SSD_HEADS):
        e[h, h * SSD_HEAD_DIM:(h + 1) * SSD_HEAD_DIM] = 1.0
    return jnp.asarray(e, BF16)


def _pad_lanes(v):
    return jnp.pad(v.astype(F32), (0, DT_LANES - v.shape[0]))[None, :]


def kernel(x, meta, w_in, conv_a, ssd_conv_w, ssd_conv_b, ssd_dt_bias, ssd_a_log, ssd_d, ssd_norm,
           w_branch, w_out, w_ffn_in, w_ffn_out, norm_mix_pre, norm_mix_post, norm_ffn_pre,
           norm_ffn_post):
    batch, seq, _ = x.shape
    assert batch == 1 and seq % CHUNK == 0
    depth = w_in.shape[0]
    n_rows = seq + CHUNK
    h = jnp.concatenate([jnp.zeros((PAD, D_MODEL), F32), meta.astype(F32), x[0].astype(F32)], axis=0)
    rope = _rope_tables(n_rows // CHUNK)
    expand = _head_expand_matrix()
    dt_end = DT_COL_START + SSD_HEADS
    w_main = jnp.concatenate([w_in[:, :, :DT_COL_START], w_in[:, :, dt_end:]], axis=2).astype(BF16)
    w_dt = jnp.pad(w_in[:, :, DT_COL_START:dt_end], ((0, 0), (0, 0), (0, DT_LANES - SSD_HEADS))).astype(BF16)
    w_branch_b, w_out_b = w_branch.astype(BF16), w_out.astype(BF16)
    w_ffn_in_b, w_ffn_out_b = w_ffn_in.astype(BF16), w_ffn_out.astype(BF16)
    for l in range(depth):
        proj, dt_raw = _inproj(h, norm_mix_pre[l][None, :], w_main, w_dt, l)
        ya, yb, yc, yd = _mixers_sb(
            proj, dt_raw, rope, conv_a[l], ssd_conv_w[l], ssd_conv_b[l][None, :],
            _pad_lanes(ssd_dt_bias[l]), _pad_lanes(ssd_a_log[l]),
            jnp.repeat(ssd_d[l].astype(F32), SSD_HEAD_DIM)[None, :], ssd_norm[l][None, :], expand)
        h = _merge(ya, yb, yc, yd, proj, h, w_branch_b, w_out_b, norm_mix_post[l][None, :], l)
        h = _ffn(h, w_ffn_in_b, w_ffn_out_b, norm_ffn_pre[l][None, :], norm_ffn_post[l][None, :], l,
                 drop_first_chunk=(l == depth - 1))
    return h[None].astype(x.dtype)
```
